```python
import math
import jax, jax.numpy as jnp
from jax import lax
import numpy as np

D_MODEL = 2048
BATCH = 2
SEQ = 4096
DEPTH = 2
DEC_BATCH = 32
DEC_SEQ = 4
PAST_LEN = 8192
PAGE_SIZE = 128

N_MIXERS = 2
N_CONV_LAYERS = (DEPTH + 1) // 2
N_ATTN_LAYERS = DEPTH // 2
CONV_WIDTH = 3
N_HEADS = 16
HEAD_DIM = D_MODEL // N_HEADS // 2
V_DIM = 2 * HEAD_DIM
D_FF = ((8 * D_MODEL // 3 + 127) // 128) * 128
N_ADA = 9
ROPE_THETA = 10000.0
Q_BLOCK = 128
EPS = 1e-5
NEG_INF = -1e30

kernel_name = 'conditioned_macaron_shortconv_diffattn_decoder_step'


def lambda_init_fn(layer_idx):
    return 0.8 - 0.6 * math.exp(-0.3 * layer_idx)


def rmsnorm(x, g):
    xf = x.astype(jnp.float32)
    y = xf * lax.rsqrt(jnp.mean(xf * xf, axis=-1, keepdims=True) + EPS)
    return (y * g.astype(jnp.float32)).astype(x.dtype)


def modulate(h, shift, scale):
    return h * (1 + scale[:, None, :]) + shift[:, None, :]


def swiglu(h, w_gate, w_up, w_down):
    return (jax.nn.silu(h @ w_gate) * (h @ w_up)) @ w_down


def rope(x, pos):
    half = HEAD_DIM // 2
    inv = ROPE_THETA ** (-jnp.arange(half, dtype=jnp.float32) / half)
    ang = pos.astype(jnp.float32)[:, None] * inv[None, :]
    cos = jnp.cos(ang)[None, :, None, :]
    sin = jnp.sin(ang)[None, :, None, :]
    xf = x.astype(jnp.float32)
    x1, x2 = xf[..., :half], xf[..., half:]
    return jnp.concatenate([x1 * cos - x2 * sin, x2 * cos + x1 * sin], axis=-1).astype(x.dtype)


def short_conv_mixer(h, conv_prefix, w_in, conv_w, w_out):
    b_gate, c_gate, v = jnp.split(h @ w_in, 3, axis=-1)
    u = c_gate * v
    buf = jnp.concatenate([conv_prefix.astype(u.dtype), u], axis=1)
    T = u.shape[1]
    conv = sum(conv_w[k] * buf[:, k:k + T] for k in range(CONV_WIDTH))
    y = (b_gate * conv) @ w_out
    return y, buf[:, -(CONV_WIDTH - 1):]


def diff_attend(q, qpos, segments, lam):
    scores = []
    for k, _, kpos in segments:
        s = jnp.einsum('bqnd,bknd->bnqk', q, k).astype(jnp.float32) * (HEAD_DIM ** -0.5)
        mask = kpos[None, :] <= qpos[:, None]
        scores.append(jnp.where(mask[None, None], s, NEG_INF))
    s = jnp.concatenate(scores, axis=-1)
    p = jax.nn.softmax(s, axis=-1)
    B, _, Tq, Tk = p.shape
    p = p.reshape(B, N_HEADS, 2, Tq, Tk)
    a = (p[:, :, 0] - lam * p[:, :, 1]).astype(q.dtype)
    out = None
    off = 0
    for _, v, _ in segments:
        n = v.shape[1]
        o = jnp.einsum('bhqk,bkhe->bqhe', a[..., off:off + n], v)
        out = o if out is None else out + o
        off += n
    return out


def diff_attn_mixer(h, pos, cache_k_l, cache_v_l, page_table, w_qkv, w_out,
                    lq1, lk1, lq2, lk2, subln_g, lambda_init):
    B, T, _ = h.shape
    q, k, v = jnp.split(h @ w_qkv, 3, axis=-1)
    q = rope(q.reshape(B, T, 2 * N_HEADS, HEAD_DIM), pos)
    k = rope(k.reshape(B, T, 2 * N_HEADS, HEAD_DIM), pos)
    v = v.reshape(B, T, N_HEADS, V_DIM)
    lam = (jnp.exp(jnp.sum(lq1.astype(jnp.float32) * lk1.astype(jnp.float32)))
           - jnp.exp(jnp.sum(lq2.astype(jnp.float32) * lk2.astype(jnp.float32)))
           + lambda_init)
    if cache_k_l is None:
        nb = T // Q_BLOCK
        qb = jnp.moveaxis(q.reshape(B, nb, Q_BLOCK, 2 * N_HEADS, HEAD_DIM), 1, 0)
        pb = pos.reshape(nb, Q_BLOCK)
        o = lax.map(lambda qp: diff_attend(qp[0], qp[1], [(k, v, pos)], lam), (qb, pb))
        o = jnp.moveaxis(o, 0, 1).reshape(B, T, N_HEADS, V_DIM)
    else:
        k_past = cache_k_l[page_table].reshape(B, -1, 2 * N_HEADS, HEAD_DIM).astype(k.dtype)
        v_past = cache_v_l[page_table].reshape(B, -1, N_HEADS, V_DIM).astype(v.dtype)
        past_pos = jnp.arange(k_past.shape[1], dtype=pos.dtype)
        o = diff_attend(q, pos, [(k_past, v_past, past_pos), (k, v, pos)], lam)
    o = rmsnorm(o, subln_g) * (1.0 - lambda_init)
    y = o.reshape(B, T, D_MODEL) @ w_out
    return y, k, v


def run_trunk(x, c, pos, conv_prefixes, cache_k, cache_v, page_table, p):
    conv_states, ks, vs = [], [], []
    for i in range(DEPTH):
        j = i // N_MIXERS
        mods = jax.nn.silu(c) @ p['w_ada'][i] + p['b_ada'][i]
        sh1, sc1, g1, sh2, sc2, g2, sh3, sc3, g3 = jnp.split(mods, N_ADA, axis=-1)
        h = modulate(rmsnorm(x, p['norm_g'][i, 0]), sh1, sc1)
        x = x + 0.5 * g1[:, None, :] * swiglu(h, p['ffn_w_gate'][i, 0], p['ffn_w_up'][i, 0], p['ffn_w_down'][i, 0])
        h = modulate(rmsnorm(x, p['norm_g'][i, 1]), sh2, sc2)
        if i % N_MIXERS == 0:
            y, st = short_conv_mixer(h, conv_prefixes[j], p['conv_w_in'][j], p['conv_w'][j], p['conv_w_out'][j])
            conv_states.append(st)
        else:
            ck = None if cache_k is None else cache_k[j]
            cv = None if cache_v is None else cache_v[j]
            y, k, v = diff_attn_mixer(h, pos, ck, cv, page_table, p['attn_w_qkv'][j], p['attn_w_out'][j],
                                      p['lambda_q1'][j], p['lambda_k1'][j], p['lambda_q2'][j], p['lambda_k2'][j],
                                      p['subln_g'][j], lambda_init_fn(i))
            ks.append(k)
            vs.append(v)
        x = x + g2[:, None, :] * y
        h = modulate(rmsnorm(x, p['norm_g'][i, 2]), sh3, sc3)
        x = x + 0.5 * g3[:, None, :] * swiglu(h, p['ffn_w_gate'][i, 1], p['ffn_w_up'][i, 1], p['ffn_w_down'][i, 1])
    return rmsnorm(x, p['final_g']), conv_states, ks, vs


def setup_inputs(seed: int = 0) -> dict:
    key = jax.random.key(seed)
    ks = jax.random.split(key, 26)
    f32 = jnp.float32
    n_pages = PAST_LEN // PAGE_SIZE
    n_used = DEC_BATCH * n_pages
    n_pool = n_used + n_used // 4
    nrm = lambda k, shape, s: jax.random.normal(k, shape, f32) * s
    page_table = jax.random.permutation(ks[7], n_pool)[:n_used].reshape(DEC_BATCH, n_pages).astype(jnp.int32)
    return {
        'x_prompt': nrm(ks[0], (BATCH, SEQ, D_MODEL), 1.0),
        'x_sample': nrm(ks[1], (DEC_BATCH, DEC_SEQ, D_MODEL), 1.0),
        'c_prompt': nrm(ks[2], (BATCH, D_MODEL), 1.0),
        'c_sample': nrm(ks[3], (DEC_BATCH, D_MODEL), 1.0),
        'state_conv': nrm(ks[4], (N_CONV_LAYERS, DEC_BATCH, CONV_WIDTH - 1, D_MODEL), 1.0),
        'cache_k': nrm(ks[5], (N_ATTN_LAYERS, n_pool, PAGE_SIZE, 2 * N_HEADS, HEAD_DIM), 1.0),
        'cache_v': nrm(ks[6], (N_ATTN_LAYERS, n_pool, PAGE_SIZE, N_HEADS, V_DIM), 1.0),
        'page_table': page_table,
        'norm_g': 1.0 + nrm(ks[8], (DEPTH, 3, D_MODEL), 0.02),
        'final_g': 1.0 + nrm(ks[9], (D_MODEL,), 0.02),
        'w_ada': nrm(ks[10], (DEPTH, D_MODEL, N_ADA * D_MODEL), D_MODEL ** -0.5),
        'b_ada': nrm(ks[11], (DEPTH, N_ADA * D_MODEL), 0.02),
        'ffn_w_gate': nrm(ks[12], (DEPTH, 2, D_MODEL, D_FF), D_MODEL ** -0.5),
        'ffn_w_up': nrm(ks[13], (DEPTH, 2, D_MODEL, D_FF), D_MODEL ** -0.5),
        'ffn_w_down': nrm(ks[14], (DEPTH, 2, D_FF, D_MODEL), D_FF ** -0.5),
        'conv_w_in': nrm(ks[15], (N_CONV_LAYERS, D_MODEL, 3 * D_MODEL), D_MODEL ** -0.5),
        'conv_w': nrm(ks[16], (N_CONV_LAYERS, CONV_WIDTH, D_MODEL), CONV_WIDTH ** -0.5),
        'conv_w_out': nrm(ks[17], (N_CONV_LAYERS, D_MODEL, D_MODEL), D_MODEL ** -0.5),
        'attn_w_qkv': nrm(ks[18], (N_ATTN_LAYERS, D_MODEL, 3 * D_MODEL), D_MODEL ** -0.5),
        'attn_w_out': nrm(ks[19], (N_ATTN_LAYERS, D_MODEL, D_MODEL), D_MODEL ** -0.5),
        'lambda_q1': nrm(ks[20], (N_ATTN_LAYERS, HEAD_DIM), 0.1),
        'lambda_k1': nrm(ks[21], (N_ATTN_LAYERS, HEAD_DIM), 0.1),
        'lambda_q2': nrm(ks[22], (N_ATTN_LAYERS, HEAD_DIM), 0.1),
        'lambda_k2': nrm(ks[23], (N_ATTN_LAYERS, HEAD_DIM), 0.1),
        'subln_g': 1.0 + nrm(ks[24], (N_ATTN_LAYERS, V_DIM), 0.02),
    }


def reference(x_prompt, x_sample, c_prompt, c_sample, state_conv, cache_k, cache_v, page_table,
              norm_g, final_g, w_ada, b_ada, ffn_w_gate, ffn_w_up, ffn_w_down,
              conv_w_in, conv_w, conv_w_out, attn_w_qkv, attn_w_out,
              lambda_q1, lambda_k1, lambda_q2, lambda_k2, subln_g):
    params = dict(norm_g=norm_g, final_g=final_g, w_ada=w_ada, b_ada=b_ada,
                  ffn_w_gate=ffn_w_gate, ffn_w_up=ffn_w_up, ffn_w_down=ffn_w_down,
                  conv_w_in=conv_w_in, conv_w=conv_w, conv_w_out=conv_w_out,
                  attn_w_qkv=attn_w_qkv, attn_w_out=attn_w_out,
                  lambda_q1=lambda_q1, lambda_k1=lambda_k1, lambda_q2=lambda_q2, lambda_k2=lambda_k2,
                  subln_g=subln_g)
    B, T = x_prompt.shape[0], x_prompt.shape[1]
    Bs, Ts = x_sample.shape[0], x_sample.shape[1]
    pos_prompt = jnp.arange(T, dtype=jnp.int32)
    pos_sample = cache_k.shape[2] * page_table.shape[1] + jnp.arange(Ts, dtype=jnp.int32)
    zero_prefix = [jnp.zeros((B, CONV_WIDTH - 1, D_MODEL), x_prompt.dtype) for _ in range(N_CONV_LAYERS)]
    y_prompt, cs_p, ks_p, vs_p = run_trunk(x_prompt, c_prompt, pos_prompt, zero_prefix,
                                           None, None, None, params)
    sample_prefix = [state_conv[j] for j in range(N_CONV_LAYERS)]
    y_sample, cs_s, ks_s, vs_s = run_trunk(x_sample, c_sample, pos_sample, sample_prefix,
                                           cache_k, cache_v, page_table, params)
    conv_state_prompt = jnp.stack(cs_p)
    conv_state_sample = jnp.stack(cs_s)
    k_prompt = jnp.stack(ks_p)
    v_prompt = jnp.stack(vs_p)
    k_sample = jnp.stack(ks_s)
    v_sample = jnp.stack(vs_s)
    return (y_prompt, y_sample, conv_state_prompt, conv_state_sample, k_prompt, v_prompt, k_sample, v_sample)
```

```python
import functools
import math

import jax
import jax.numpy as jnp
from jax import lax
from jax.experimental import pallas as pl
from jax.experimental.pallas import tpu as pltpu

F32 = jnp.float32
BF16 = jnp.bfloat16

D_MODEL = 2048
DEPTH = 2
N_HEADS = 16
HEAD_DIM = 64
V_DIM = 128
D_FF = 5504
N_ADA = 9
CONV_WIDTH = 3
ROPE_THETA = 10000.0
EPS = 1e-5
NEG_INF = -1e30
PAGE_SIZE = 128

LANES = 128
SUBLANES = 8
FF_TILE = 512
FF_PAD = ((D_FF + FF_TILE - 1) // FF_TILE) * FF_TILE
PROMPT_TILE = 1024
MIX_TILE = 256
QKV_TILE = 512
PROJ_TILE = 1024
ADA_TILE = 1024
ATTN_TILE = 512
PAGES_PER_STEP = 8
VMEM_LIMIT = 60000 * 1024


def _lambda_init(layer_idx):
    return 0.8 - 0.6 * math.exp(-0.3 * layer_idx)


class _Stream:
    def __init__(self, rows, tile, tiles_per_seq, mod_rows):
        self.rows = rows
        self.tile = tile
        self.tiles_per_seq = tiles_per_seq
        self.mod_rows = mod_rows


def _params(sem):
    return pltpu.CompilerParams(dimension_semantics=sem, vmem_limit_bytes=VMEM_LIMIT)


def _silu(x):
    return x * jax.nn.sigmoid(x)


def _rmsnorm(x, g):
    return x * lax.rsqrt(jnp.mean(x * x, axis=-1, keepdims=True) + EPS) * g


def _norm_mod(x, g, shift, scale):
    return _rmsnorm(x, g) * (1 + scale) + shift


def _dot(a, b):
    return jnp.dot(a, b, preferred_element_type=F32)


def _ada_kernel(c_ref, w_ref, b_ref, o_ref):
    a = _silu(c_ref[...]).astype(BF16)
    o_ref[...] = _dot(a, w_ref[...].astype(BF16)) + b_ref[...]


def _ada_call(c_all, w_ada, b_ada):
    rows = c_all.shape[0]
    n_out = N_ADA * D_MODEL
    return pl.pallas_call(
        _ada_kernel,
        out_shape=jax.ShapeDtypeStruct((DEPTH, rows, n_out), F32),
        grid=(DEPTH, n_out // ADA_TILE),
        in_specs=[
            pl.BlockSpec((rows, D_MODEL), lambda l, j: (0, 0)),
            pl.BlockSpec((None, D_MODEL, ADA_TILE), lambda l, j: (l, 0, j)),
            pl.BlockSpec((None, 1, ADA_TILE), lambda l, j: (l, 0, j)),
        ],
        out_specs=pl.BlockSpec((None, rows, ADA_TILE), lambda l, j: (l, 0, j)),
        compiler_params=_params(("arbitrary", "arbitrary")),
        name="ada_proj",
    )(c_all, w_ada, b_ada.reshape(DEPTH, 1, n_out))


def _x_spec(st, buffers=None):
    mode = None if buffers is None else pl.Buffered(buffers)
    return pl.BlockSpec((st.tile, D_MODEL), lambda i, j: (i, 0), pipeline_mode=mode)


def _norm_spec(layer, sub):
    return pl.BlockSpec((None, 1, D_MODEL), lambda i, j: (layer * 3 + sub, 0, 0))


def _mod_spec(st, k, width=D_MODEL, by_col=False):
    tps = st.tiles_per_seq
    if by_col:
        return pl.BlockSpec((None, None, st.mod_rows, width), lambda i, j: (k, i // tps, 0, j))
    return pl.BlockSpec((None, None, st.mod_rows, width), lambda i, j: (k, i // tps, 0, 0))


def _ffn_kernel(x_ref, g_ref, sh_ref, sc_ref, gt_ref, wg_ref, wu_ref, wd_ref, *rest, final):
    if final:
        fg_ref, o_ref, h_ref = rest
    else:
        o_ref, h_ref = rest
    j = pl.program_id(1)

    @pl.when(j == 0)
    def _():
        x = x_ref[...]
        h_ref[...] = _norm_mod(x, g_ref[...], sh_ref[...], sc_ref[...]).astype(BF16)
        o_ref[...] = x

    h = h_ref[...]
    a = (_silu(_dot(h, wg_ref[...])) * _dot(h, wu_ref[...])).astype(BF16)
    o_ref[...] += _dot(a, wd_ref[...]) * (0.5 * gt_ref[...])

    if final:
        @pl.when(j == pl.num_programs(1) - 1)
        def _():
            o_ref[...] = _rmsnorm(o_ref[...], fg_ref[...])


def _ffn_call(st, x, mods, norm_g, wg, wu, wd, layer, which, final_g=None):
    sub = 2 * which
    in_specs = [
        _x_spec(st, 1),
        _norm_spec(layer, sub),
        _mod_spec(st, 3 * sub), _mod_spec(st, 3 * sub + 1), _mod_spec(st, 3 * sub + 2),
        pl.BlockSpec((None, None, D_MODEL, FF_TILE), lambda i, j: (layer, which, 0, j)),
        pl.BlockSpec((None, None, D_MODEL, FF_TILE), lambda i, j: (layer, which, 0, j)),
        pl.BlockSpec((None, None, FF_TILE, D_MODEL), lambda i, j: (layer, which, j, 0)),
    ]
    args = [x, norm_g, mods, mods, mods, wg, wu, wd]
    if final_g is not None:
        in_specs.append(pl.BlockSpec((1, D_MODEL), lambda i, j: (0, 0)))
        args.append(final_g.reshape(1, D_MODEL))
    return pl.pallas_call(
        functools.partial(_ffn_kernel, final=final_g is not None),
        out_shape=jax.ShapeDtypeStruct((st.rows, D_MODEL), F32),
        grid=(st.rows // st.tile, FF_PAD // FF_TILE),
        in_specs=in_specs,
        out_specs=_x_spec(st),
        scratch_shapes=[pltpu.VMEM((st.tile, D_MODEL), BF16)],
        compiler_params=_params(("arbitrary", "arbitrary")),
        name="ffn",
    )(*args)


def _conv_kernel(x_ref, g_ref, sh_ref, sc_ref, gt_ref, wb_ref, wc_ref, wv_ref, cw_ref, wo_ref,
                 *rest, seg_rows, tiles_per_seq):
    if seg_rows is None:
        o_ref, st_ref, h_ref, ubuf_ref, carry_ref = rest
    else:
        p1_ref, p2_ref, o_ref, st_ref, h_ref, ubuf_ref = rest
    i = pl.program_id(0)
    j = pl.program_id(1)
    tm = x_ref.shape[0]

    @pl.when(j == 0)
    def _():
        x = x_ref[...]
        h_ref[...] = _norm_mod(x, g_ref[...], sh_ref[...], sc_ref[...]).astype(BF16)
        o_ref[...] = x

    h = h_ref[...]
    u = _dot(h, wc_ref[...]) * _dot(h, wv_ref[...])
    ubuf_ref[SUBLANES:, :] = u
    if seg_rows is None:
        prev = carry_ref[j]
        ubuf_ref[:SUBLANES, :] = jnp.where(i % tiles_per_seq == 0, jnp.zeros_like(prev), prev)
        tail = u[tm - SUBLANES:, :]
        carry_ref[j] = tail
        st_ref[...] = tail
        prev1 = ubuf_ref[SUBLANES - 1:SUBLANES - 1 + tm, :]
        prev2 = ubuf_ref[SUBLANES - 2:SUBLANES - 2 + tm, :]
    else:
        ubuf_ref[:SUBLANES, :] = jnp.zeros((SUBLANES, u.shape[1]), F32)
        st_ref[...] = u
        t = lax.broadcasted_iota(jnp.int32, u.shape, 0) % seg_rows
        prev1 = jnp.where(t >= 1, ubuf_ref[SUBLANES - 1:SUBLANES - 1 + tm, :], p1_ref[...])
        prev2 = jnp.where(t >= 2, ubuf_ref[SUBLANES - 2:SUBLANES - 2 + tm, :], p2_ref[...])
    conv = cw_ref[0:1, :] * prev2 + cw_ref[1:2, :] * prev1 + cw_ref[2:3, :] * u
    gated = (_dot(h, wb_ref[...]) * conv).astype(BF16)
    o_ref[...] += _dot(gated, wo_ref[...]) * gt_ref[...]


def _conv_call(st, x, mods, norm_g, w_in, conv_w, w_out, layer, j_conv, prefix=None, seg_rows=None):
    tn = MIX_TILE
    nj = D_MODEL // tn
    col = pl.BlockSpec((st.tile, tn), lambda i, j: (i, j))
    in_specs = [
        _x_spec(st, 1),
        _norm_spec(layer, 1),
        _mod_spec(st, 3), _mod_spec(st, 4), _mod_spec(st, 5),
        pl.BlockSpec((None, D_MODEL, tn), lambda i, j: (j_conv, 0, j)),
        pl.BlockSpec((None, D_MODEL, tn), lambda i, j: (j_conv, 0, nj + j)),
        pl.BlockSpec((None, D_MODEL, tn), lambda i, j: (j_conv, 0, 2 * nj + j)),
        pl.BlockSpec((None, CONV_WIDTH, tn), lambda i, j: (j_conv, 0, j)),
        pl.BlockSpec((None, tn, D_MODEL), lambda i, j: (j_conv, j, 0)),
    ]
    args = [x, norm_g, mods, mods, mods, w_in, w_in, w_in, conv_w, w_out]
    scratch = [pltpu.VMEM((st.tile, D_MODEL), BF16), pltpu.VMEM((st.tile + SUBLANES, tn), F32)]
    if prefix is None:
        tps = st.tiles_per_seq
        n_seq = st.rows // (st.tile * tps)
        st_shape = jax.ShapeDtypeStruct((n_seq, SUBLANES, D_MODEL), F32)
        st_spec = pl.BlockSpec((None, SUBLANES, tn), lambda i, j: (i // tps, 0, j))
        scratch.append(pltpu.VMEM((nj, SUBLANES, tn), F32))
    else:
        in_specs += [col, col]
        args += list(prefix)
        st_shape = jax.ShapeDtypeStruct((st.rows, D_MODEL), F32)
        st_spec = col
    return pl.pallas_call(
        functools.partial(_conv_kernel, seg_rows=seg_rows, tiles_per_seq=st.tiles_per_seq),
        out_shape=(jax.ShapeDtypeStruct((st.rows, D_MODEL), F32), st_shape),
        grid=(st.rows // st.tile, nj),
        in_specs=in_specs,
        out_specs=(_x_spec(st), st_spec),
        scratch_shapes=scratch,
        compiler_params=_params(("arbitrary", "arbitrary")),
        name="conv_mixer",
    )(*args)


def _rope_tile(x, cos, sin, first_half):
    outs = []
    for c in range(x.shape[1] // LANES):
        xc = x[:, c * LANES:(c + 1) * LANES]
        partner = jnp.where(first_half, pltpu.roll(xc, LANES - HEAD_DIM // 2, 1),
                            pltpu.roll(xc, HEAD_DIM // 2, 1))
        outs.append(xc * cos + partner * sin)
    return jnp.concatenate(outs, axis=1)


def _qkv_kernel(x_ref, g_ref, sh_ref, sc_ref, wq_ref, wk_ref, wv_ref, cos_ref, sin_ref,
                q_ref, k_ref, v_ref, kf_ref, vf_ref, h_ref):
    @pl.when(pl.program_id(1) == 0)
    def _():
        h_ref[...] = _norm_mod(x_ref[...], g_ref[...], sh_ref[...], sc_ref[...]).astype(BF16)

    h = h_ref[...]
    cos = cos_ref[...]
    sin = sin_ref[...]
    first_half = lax.broadcasted_iota(jnp.int32, cos.shape, 1) % HEAD_DIM < HEAD_DIM // 2
    q = _rope_tile(_dot(h, wq_ref[...]), cos, sin, first_half)
    k = _rope_tile(_dot(h, wk_ref[...]), cos, sin, first_half)
    v = _dot(h, wv_ref[...])
    q_ref[...] = q.astype(BF16)
    k_ref[...] = k.astype(BF16)
    v_ref[...] = v.astype(BF16)
    kf_ref[...] = k
    vf_ref[...] = v


def _qkv_call(st, x, mods, norm_g, w_qkv, cos_t, sin_t, layer, j_attn):
    tn = QKV_TILE
    nj = D_MODEL // tn
    tps = st.tiles_per_seq
    col = pl.BlockSpec((st.tile, tn), lambda i, j: (i, j))
    rope = pl.BlockSpec((st.tile, LANES), lambda i, j: (i % tps, 0))
    bf = jax.ShapeDtypeStruct((st.rows, D_MODEL), BF16)
    f32 = jax.ShapeDtypeStruct((st.rows, D_MODEL), F32)
    return pl.pallas_call(
        _qkv_kernel,
        out_shape=(bf, bf, bf, f32, f32),
        grid=(st.rows // st.tile, nj),
        in_specs=[
            _x_spec(st, 1),
            _norm_spec(layer, 1),
            _mod_spec(st, 3), _mod_spec(st, 4),
            pl.BlockSpec((None, D_MODEL, tn), lambda i, j: (j_attn, 0, j)),
            pl.BlockSpec((None, D_MODEL, tn), lambda i, j: (j_attn, 0, nj + j)),
            pl.BlockSpec((None, D_MODEL, tn), lambda i, j: (j_attn, 0, 2 * nj + j)),
            rope, rope,
        ],
        out_specs=(col, col, col, col, col),
        scratch_shapes=[pltpu.VMEM((st.tile, D_MODEL), BF16)],
        compiler_params=_params(("arbitrary", "arbitrary")),
        name="qkv_rope",
    )(x, norm_g, mods, mods, w_qkv, w_qkv, w_qkv, cos_t, sin_t)


def _proj_kernel(a_ref, w_ref, x_ref, gt_ref, o_ref):
    o_ref[...] = x_ref[...] + gt_ref[...] * _dot(a_ref[...], w_ref[...])


def _proj_call(st, a, w_out, x, mods, j_attn):
    tn = PROJ_TILE
    col = pl.BlockSpec((st.tile, tn), lambda i, j: (i, j))
    return pl.pallas_call(
        _proj_kernel,
        out_shape=jax.ShapeDtypeStruct((st.rows, D_MODEL), F32),
        grid=(st.rows // st.tile, D_MODEL // tn),
        in_specs=[
            pl.BlockSpec((st.tile, D_MODEL), lambda i, j: (i, 0)),
            pl.BlockSpec((None, D_MODEL, tn), lambda i, j: (j_attn, 0, j)),
            col,
            _mod_spec(st, 5, width=tn, by_col=True),
        ],
        out_specs=col,
        compiler_params=_params(("arbitrary", "arbitrary")),
        name="out_proj",
    )(a, w_out, x, mods)


def _subln(o, g, lam_init):
    return o * lax.rsqrt(jnp.mean(o * o, axis=-1, keepdims=True) + EPS) * g * (1.0 - lam_init)


def _flash_kernel(qi_ref, ki_ref, lam_ref, q_ref, k_ref, v_ref, sg_ref, o_ref,
                  q2_ref, m_ref, l_ref, acc_ref, *, lam_init):
    s = pl.program_id(2)
    qi = qi_ref[s]
    ki = ki_ref[s]
    tq = q_ref.shape[0]
    tk = k_ref.shape[0]

    @pl.when(ki == 0)
    def _():
        q = q_ref[...] * (HEAD_DIM ** -0.5)
        lane = lax.broadcasted_iota(jnp.int32, q.shape, 1)
        zero = jnp.zeros_like(q)
        q2_ref[:tq, :] = jnp.where(lane < HEAD_DIM, q, zero)
        q2_ref[tq:, :] = jnp.where(lane >= HEAD_DIM, q, zero)
        m_ref[...] = jnp.full(m_ref.shape, NEG_INF, F32)
        l_ref[...] = jnp.zeros(l_ref.shape, F32)
        acc_ref[...] = jnp.zeros(acc_ref.shape, F32)

    def step(diagonal):
        sc = lax.dot_general(q2_ref[...], k_ref[...], (((1,), (1,)), ((), ())), preferred_element_type=F32)
        if diagonal:
            row = lax.broadcasted_iota(jnp.int32, sc.shape, 0) % tq
            col = lax.broadcasted_iota(jnp.int32, sc.shape, 1)
            sc = jnp.where(col <= row, sc, NEG_INF)
        blocks = [sc[:, c * LANES:(c + 1) * LANES] for c in range(tk // LANES)]
        m_prev = m_ref[...]
        m_new = jnp.maximum(m_prev, jnp.max(functools.reduce(jnp.maximum, blocks), axis=1, keepdims=True))
        alpha = jnp.exp(m_prev - m_new)
        ps = [jnp.exp(b - m_new) for b in blocks]
        l_ref[...] = alpha * l_ref[...] + functools.reduce(jnp.add, ps)
        p = jnp.concatenate(ps, axis=1).astype(BF16)
        acc_ref[...] = alpha * acc_ref[...] + _dot(p, v_ref[...])
        m_ref[...] = m_new

    @pl.when(ki != qi)
    def _():
        step(False)

    @pl.when(ki == qi)
    def _():
        step(True)
        o = acc_ref[...] / jnp.sum(l_ref[...], axis=1, keepdims=True)
        o = o[:tq] - lam_ref[0] * o[tq:]
        o_ref[...] = _subln(o, sg_ref[...], lam_init).astype(BF16)


def _flash_call(q, k, v, lam, subln_g, n_seq, seq_len, lam_init):
    t = ATTN_TILE
    nb = seq_len // t
    pairs = [(a, b) for a in range(nb) for b in range(a + 1)]
    qi_tab = jnp.asarray([p[0] for p in pairs], jnp.int32)
    ki_tab = jnp.asarray([p[1] for p in pairs], jnp.int32)
    q_spec = pl.BlockSpec((t, LANES), lambda b, h, s, qi, ki: (b * nb + qi[s], h))
    kv_spec = pl.BlockSpec((t, LANES), lambda b, h, s, qi, ki: (b * nb + ki[s], h))
    grid_spec = pltpu.PrefetchScalarGridSpec(
        num_scalar_prefetch=2,
        grid=(n_seq, N_HEADS, len(pairs)),
        in_specs=[
            pl.BlockSpec(memory_space=pltpu.SMEM),
            q_spec, kv_spec, kv_spec,
            pl.BlockSpec((1, V_DIM), lambda b, h, s, qi, ki: (0, 0)),
        ],
        out_specs=q_spec,
        scratch_shapes=[
            pltpu.VMEM((2 * t, LANES), BF16), pltpu.VMEM((2 * t, LANES), F32),
            pltpu.VMEM((2 * t, LANES), F32), pltpu.VMEM((2 * t, V_DIM), F32),
        ],
    )
    return pl.pallas_call(
        functools.partial(_flash_kernel, lam_init=lam_init),
        out_shape=jax.ShapeDtypeStruct(q.shape, BF16),
        grid_spec=grid_spec,
        compiler_params=_params(("arbitrary", "arbitrary", "arbitrary")),
        name="flash_diff_attn",
    )(qi_tab, ki_tab, lam, q, k, v, subln_g)


def _paged_kernel(pt_ref, lam_ref, qbd_ref, kn_ref, vn_ref, e_ref, hm_ref, sg_ref, *rest,
                  n_pages, t_new, lam_init):
    pp = PAGES_PER_STEP
    k_refs = rest[:pp]
    v_refs = rest[pp:2 * pp]
    o_ref, s_ref, a_ref, acc_ref = rest[2 * pp:]
    step = pl.program_id(1)
    k_steps = n_pages // pp
    past = n_pages * PAGE_SIZE
    scale = HEAD_DIM ** -0.5
    n_rows = N_HEADS * t_new

    @pl.when(step < k_steps)
    def _():
        qbd = qbd_ref[...]
        for i in range(pp):
            col = pl.multiple_of((step * pp + i) * PAGE_SIZE, PAGE_SIZE)
            s_ref[:, pl.ds(col, PAGE_SIZE)] = _dot(qbd, k_refs[i][...].astype(BF16)) * scale

    @pl.when(step == k_steps - 1)
    def _():
        sn = lax.dot_general(qbd_ref[...], kn_ref[...], (((1,), (1,)), ((), ())),
                             preferred_element_type=F32) * scale
        tq = lax.broadcasted_iota(jnp.int32, sn.shape, 0) % t_new
        tk = lax.broadcasted_iota(jnp.int32, sn.shape, 1)
        s_ref[:, past:] = jnp.where(tk <= tq, sn, NEG_INF)
        sc = s_ref[...]
        p = jnp.exp(sc - jnp.max(sc, axis=1, keepdims=True))
        p = p * (1.0 / jnp.sum(p, axis=1, keepdims=True))
        a_ref[...] = (p[:n_rows] - lam_ref[0] * p[n_rows:]).astype(BF16)
        acc_ref[...] = jnp.zeros(acc_ref.shape, F32)

    def weighted_values(a, values):
        spread = (_dot(a, e_ref[...]) * hm_ref[...]).astype(BF16)
        return _dot(spread[:, :values.shape[0]], values)

    @pl.when(step >= k_steps)
    def _():
        acc = acc_ref[...]
        for i in range(pp):
            col = pl.multiple_of(((step - k_steps) * pp + i) * PAGE_SIZE, PAGE_SIZE)
            acc += weighted_values(a_ref[:, pl.ds(col, PAGE_SIZE)], v_refs[i][...].astype(BF16))
        acc_ref[...] = acc

    @pl.when(step == 2 * k_steps - 1)
    def _():
        o = acc_ref[...] + weighted_values(a_ref[:, past:], vn_ref[...])
        o_ref[...] = _subln(o, sg_ref[...], lam_init).astype(BF16)


def _paged_call(page_table, lam, qbd, k_new, v_new, spread, head_mask, subln_g, k_pages, v_pages, lam_init):
    n_seq, n_pages = page_table.shape
    pp = PAGES_PER_STEP
    k_steps = n_pages // pp
    n_rows = head_mask.shape[0]
    t_new = n_rows // N_HEADS
    keys = n_pages * PAGE_SIZE + LANES
    page_rows = k_pages.shape[1]

    def k_spec(i):
        return pl.BlockSpec(
            (None, page_rows, PAGE_SIZE),
            lambda b, s, pt: (pt[b * n_pages + jnp.minimum(s, k_steps - 1) * pp + i], 0, 0))

    def v_spec(i):
        return pl.BlockSpec(
            (None, page_rows, V_DIM),
            lambda b, s, pt: (pt[b * n_pages + jnp.maximum(s - k_steps, 0) * pp + i], 0, 0))

    def whole(shape):
        return pl.BlockSpec(shape, lambda b, s, pt: (0,) * len(shape))

    def per_seq(shape):
        return pl.BlockSpec((None,) + shape, lambda b, s, pt: (b,) + (0,) * len(shape))

    grid_spec = pltpu.PrefetchScalarGridSpec(
        num_scalar_prefetch=1,
        grid=(n_seq, 2 * k_steps),
        in_specs=[
            pl.BlockSpec(memory_space=pltpu.SMEM),
            per_seq((2 * n_rows, D_MODEL)),
            per_seq((LANES, D_MODEL)),
            per_seq((LANES, V_DIM)),
            whole(spread.shape), whole(head_mask.shape), whole((1, V_DIM)),
        ] + [k_spec(i) for i in range(pp)] + [v_spec(i) for i in range(pp)],
        out_specs=per_seq((n_rows, V_DIM)),
        scratch_shapes=[
            pltpu.VMEM((2 * n_rows, keys), F32), pltpu.VMEM((n_rows, keys), BF16),
            pltpu.VMEM((n_rows, V_DIM), F32),
        ],
    )
    return pl.pallas_call(
        functools.partial(_paged_kernel, n_pages=n_pages, t_new=t_new, lam_init=lam_init),
        out_shape=jax.ShapeDtypeStruct((n_seq, n_rows, V_DIM), BF16),
        grid_spec=grid_spec,
        compiler_params=_params(("arbitrary", "arbitrary")),
        name="paged_diff_attn",
    )(page_table.reshape(-1), lam, qbd, k_new, v_new, spread, head_mask, subln_g,
      *([k_pages] * pp), *([v_pages] * pp))


def _rope_tables(pos):
    half = HEAD_DIM // 2
    inv = ROPE_THETA ** (-jnp.arange(half, dtype=F32) / half)
    ang = pos.astype(F32)[:, None] * inv[None, :]
    cos = jnp.cos(ang)
    sin = jnp.sin(ang)
    reps = LANES // HEAD_DIM
    return (jnp.tile(jnp.concatenate([cos, cos], axis=1), (1, reps)),
            jnp.tile(jnp.concatenate([-sin, sin], axis=1), (1, reps)))


def _sample_attention(q, k, v, page_table, lam, subln_g, k_pages, v_pages, lam_init):
    n_seq = page_table.shape[0]
    t_new = q.shape[0] // n_seq
    n_sub = 2 * N_HEADS
    head_of_lane = jnp.arange(D_MODEL) // HEAD_DIM
    sub_head = 2 * jnp.arange(N_HEADS)[None, :] + jnp.arange(2)[:, None]
    keep = (head_of_lane[None, None, :] == sub_head[:, :, None]).astype(BF16)
    q3 = q.reshape(n_seq, 1, 1, t_new, D_MODEL)
    qbd = (q3 * keep[None, :, :, None, :]).reshape(n_seq, n_sub * t_new, D_MODEL)
    k_new = jnp.pad(k.reshape(n_seq, t_new, D_MODEL), ((0, 0), (0, LANES - t_new), (0, 0)))
    v_new = jnp.pad(v.reshape(n_seq, t_new * N_HEADS, V_DIM), ((0, 0), (0, LANES - t_new * N_HEADS), (0, 0)))
    lane = jnp.arange(PAGE_SIZE * N_HEADS)
    spread = (lane[None, :] // N_HEADS == jnp.arange(PAGE_SIZE)[:, None]).astype(BF16)
    row_head = jnp.arange(N_HEADS * t_new) // t_new
    head_mask = (lane[None, :] % N_HEADS == row_head[:, None]).astype(F32)
    o = _paged_call(page_table, lam, qbd, k_new, v_new, spread, head_mask, subln_g,
                    k_pages, v_pages, lam_init)
    o = o.reshape(n_seq, N_HEADS, t_new, V_DIM).transpose(0, 2, 1, 3)
    return o.reshape(n_seq * t_new, D_MODEL)


def _trunk(st, x, mods, pos_tables, p, conv_prefix, seg_rows, attend):
    conv_state = kq = vq = None
    cos_t, sin_t = pos_tables
    for layer in range(DEPTH):
        j = layer // 2
        m = mods[layer]
        x = _ffn_call(st, x, m, p['norm_g'], p['wg'], p['wu'], p['wd'], layer, 0)
        if layer % 2 == 0:
            x, conv_state = _conv_call(st, x, m, p['norm_g'], p['conv_w_in'], p['conv_w'], p['conv_w_out'],
                                       layer, j, conv_prefix, seg_rows)
        else:
            q, k, v, kq, vq = _qkv_call(st, x, m, p['norm_g'], p['attn_w_qkv'], cos_t, sin_t, layer, j)
            o = attend(q, k, v, _lambda_init(layer))
            x = _proj_call(st, o, p['attn_w_out'], x, m, j)
        final_g = p['final_g'] if layer == DEPTH - 1 else None
        x = _ffn_call(st, x, m, p['norm_g'], p['wg'], p['wu'], p['wd'], layer, 1, final_g)
    return x, conv_state, kq, vq


def kernel(x_prompt, x_sample, c_prompt, c_sample, state_conv, cache_k, cache_v, page_table, norm_g, final_g,
           w_ada, b_ada, ffn_w_gate, ffn_w_up, ffn_w_down, conv_w_in, conv_w, conv_w_out, attn_w_qkv,
           attn_w_out, lambda_q1, lambda_k1, lambda_q2, lambda_k2, subln_g):
    n_prompt, seq_len, _ = x_prompt.shape
    n_sample, t_new, _ = x_sample.shape
    n_pages = page_table.shape[1]
    assert DEPTH == 2 and t_new >= CONV_WIDTH - 1 and seq_len % PROMPT_TILE == 0
    ff_pad = FF_PAD - D_FF
    p = dict(
        norm_g=norm_g.reshape(DEPTH * 3, 1, D_MODEL),
        final_g=final_g,
        wg=jnp.pad(ffn_w_gate.astype(BF16), ((0, 0), (0, 0), (0, 0), (0, ff_pad))),
        wu=jnp.pad(ffn_w_up.astype(BF16), ((0, 0), (0, 0), (0, 0), (0, ff_pad))),
        wd=jnp.pad(ffn_w_down.astype(BF16), ((0, 0), (0, 0), (0, ff_pad), (0, 0))),
        conv_w_in=conv_w_in.astype(BF16),
        conv_w=conv_w,
        conv_w_out=conv_w_out.astype(BF16),
        attn_w_qkv=attn_w_qkv.astype(BF16),
        attn_w_out=attn_w_out.astype(BF16),
    )
    j_attn = 0
    lam = (jnp.exp(jnp.sum(lambda_q1[j_attn] * lambda_k1[j_attn]))
           - jnp.exp(jnp.sum(lambda_q2[j_attn] * lambda_k2[j_attn])) + _lambda_init(1)).reshape(1).astype(F32)
    sg = subln_g[j_attn].reshape(1, V_DIM)

    n_cond = n_prompt + n_sample
    c_rows = ((n_cond + SUBLANES - 1) // SUBLANES) * SUBLANES
    c_all = jnp.pad(jnp.concatenate([c_prompt, c_sample], axis=0), ((0, c_rows - n_cond), (0, 0)))
    mods = _ada_call(c_all, w_ada, b_ada).reshape(DEPTH, c_rows, N_ADA, D_MODEL)
    mods_p = [mods[l, :n_prompt].transpose(1, 0, 2)[:, :, None, :] for l in range(DEPTH)]
    mods_s = [jnp.repeat(mods[l, n_prompt:n_cond], t_new, axis=0).transpose(1, 0, 2)[:, None]
              for l in range(DEPTH)]

    st_p = _Stream(n_prompt * seq_len, PROMPT_TILE, seq_len // PROMPT_TILE, 1)
    pos_p = _rope_tables(jnp.arange(seq_len, dtype=jnp.int32))
    attend_p = lambda q, k, v, li: _flash_call(q, k, v, lam, sg, n_prompt, seq_len, li)
    y_p, cs_p, k_p, v_p = _trunk(st_p, x_prompt.reshape(-1, D_MODEL), mods_p, pos_p, p, None, None, attend_p)

    rows_s = n_sample * t_new
    st_s = _Stream(rows_s, rows_s, 1, rows_s)
    pos_s = _rope_tables(n_pages * PAGE_SIZE + jnp.arange(rows_s, dtype=jnp.int32) % t_new)
    t_idx = (jnp.arange(rows_s) % t_new)[:, None]
    pre0 = jnp.repeat(state_conv[0, :, 0], t_new, axis=0)
    pre1 = jnp.repeat(state_conv[0, :, 1], t_new, axis=0)
    prefix1 = jnp.where(t_idx == 0, pre1, 0.0)
    prefix2 = jnp.where(t_idx == 0, pre0, jnp.where(t_idx == 1, pre1, 0.0))
    k_pages = cache_k[j_attn].transpose(0, 2, 3, 1).reshape(-1, 2 * N_HEADS * HEAD_DIM, PAGE_SIZE)
    v_pages = cache_v[j_attn].reshape(-1, PAGE_SIZE * N_HEADS, V_DIM)
    attend_s = lambda q, k, v, li: _sample_attention(q, k, v, page_table, lam, sg, k_pages, v_pages, li)
    y_s, u_s, k_s, v_s = _trunk(st_s, x_sample.reshape(-1, D_MODEL), mods_s, pos_s, p,
                                (prefix1, prefix2), t_new, attend_s)

    keep = CONV_WIDTH - 1
    return (
        y_p.reshape(n_prompt, seq_len, D_MODEL),
        y_s.reshape(n_sample, t_new, D_MODEL),
        cs_p[None, :, SUBLANES - keep:, :],
        u_s.reshape(n_sample, t_new, D_MODEL)[None, :, t_new - keep:, :],
        k_p.reshape(1, n_prompt, seq_len, 2 * N_HEADS, HEAD_DIM),
        v_p.reshape(1, n_prompt, seq_len, N_HEADS, V_DIM),
        k_s.reshape(1, n_sample, t_new, 2 * N_HEADS, HEAD_DIM),
        v_s.reshape(1, n_sample, t_new, N_HEADS, V_DIM),
    )
```

```python
import functools
import math

import jax
import jax.numpy as jnp
from jax import lax
from jax.experimental import pallas as pl
from jax.experimental.pallas import tpu as pltpu

F32 = jnp.float32
BF16 = jnp.bfloat16

D_MODEL = 2048
DEPTH = 2
N_HEADS = 16
HEAD_DIM = 64
V_DIM = 128
D_FF = 5504
N_ADA = 9
CONV_WIDTH = 3
ROPE_THETA = 10000.0
EPS = 1e-5
NEG_INF = -1e30
PAGE_SIZE = 128

LANES = 128
SUBLANES = 8
FF_TILE = 512
FF_PAD = ((D_FF + FF_TILE - 1) // FF_TILE) * FF_TILE
PROMPT_TILE = 1024
MIX_ROWS = 512
MIX_TILE = 512
MIX_CHUNK = 256
QKV_TILE = 512
PROJ_TILE = 1024
ADA_TILE = 1024
CAST_TILE = 256
ATTN_TILE = 512
ATTN_CHUNK = 256
PAGES_PER_STEP = 8
VMEM_LIMIT = 60000 * 1024


def _lambda_init(layer_idx):
    return 0.8 - 0.6 * math.exp(-0.3 * layer_idx)


class _Stream:
    def __init__(self, rows, tile, tiles_per_seq, mod_rows):
        self.rows = rows
        self.tile = tile
        self.tiles_per_seq = tiles_per_seq
        self.mod_rows = mod_rows


def _params(sem):
    return pltpu.CompilerParams(dimension_semantics=sem, vmem_limit_bytes=VMEM_LIMIT)


def _silu(x):
    return x * jax.nn.sigmoid(x)


def _rmsnorm(x, g):
    return x * lax.rsqrt(jnp.mean(x * x, axis=-1, keepdims=True) + EPS) * g


def _dot(a, b):
    return jnp.dot(a, b, preferred_element_type=F32)


def _norm_mod_rows(x_ref, g_ref, sh_ref, sc_ref, h_ref, copy_ref=None):
    x = x_ref[...]
    h_ref[...] = (_rmsnorm(x, g_ref[...]) * (1 + sc_ref[...]) + sh_ref[...]).astype(BF16)
    if copy_ref is not None:
        copy_ref[...] = x


def _cast_gate_up_kernel(g_ref, u_ref, o_ref):
    rows = o_ref.shape[0]
    for c in range(FF_PAD // FF_TILE):
        lo = c * FF_TILE
        n = min(FF_TILE, D_FF - lo)
        for k, ref in enumerate((g_ref, u_ref)):
            dst = 2 * lo + k * FF_TILE
            o_ref[:, dst:dst + n] = ref[:, lo:lo + n].astype(BF16)
            if n < FF_TILE:
                o_ref[:, dst + n:dst + FF_TILE] = jnp.zeros((rows, FF_TILE - n), BF16)


def _cast_down_kernel(w_ref, o_ref):
    o_ref[:D_FF, :] = w_ref[...].astype(BF16)
    o_ref[D_FF:, :] = jnp.zeros((FF_PAD - D_FF, o_ref.shape[1]), BF16)


def _cast_ffn_weights(w_gate, w_up, w_down):
    n_l, n_w = w_gate.shape[:2]
    t = CAST_TILE
    cp = _params(("arbitrary", "arbitrary", "arbitrary"))
    rows_in = pl.BlockSpec((None, None, t, D_FF), lambda l, w, i: (l, w, i, 0))
    w_gu = pl.pallas_call(
        _cast_gate_up_kernel,
        out_shape=jax.ShapeDtypeStruct((n_l, n_w, D_MODEL, 2 * FF_PAD), BF16),
        grid=(n_l, n_w, D_MODEL // t),
        in_specs=[rows_in, rows_in],
        out_specs=pl.BlockSpec((None, None, t, 2 * FF_PAD), lambda l, w, i: (l, w, i, 0)),
        compiler_params=cp,
        name="cast_gate_up",
    )(w_gate, w_up)
    w_d = pl.pallas_call(
        _cast_down_kernel,
        out_shape=jax.ShapeDtypeStruct((n_l, n_w, FF_PAD, D_MODEL), BF16),
        grid=(n_l, n_w, D_MODEL // t),
        in_specs=[pl.BlockSpec((None, None, D_FF, t), lambda l, w, i: (l, w, 0, i))],
        out_specs=pl.BlockSpec((None, None, FF_PAD, t), lambda l, w, i: (l, w, 0, i)),
        compiler_params=cp,
        name="cast_down",
    )(w_down)
    return w_gu, w_d


def _ada_kernel(c_ref, w_ref, b_ref, o_ref):
    a = _silu(c_ref[...]).astype(BF16)
    o_ref[...] = _dot(a, w_ref[...].astype(BF16)) + b_ref[...]


def _ada_call(c_all, w_ada, b_ada):
    rows = c_all.shape[0]
    n_out = N_ADA * D_MODEL
    return pl.pallas_call(
        _ada_kernel,
        out_shape=jax.ShapeDtypeStruct((DEPTH, rows, n_out), F32),
        grid=(DEPTH, n_out // ADA_TILE),
        in_specs=[
            pl.BlockSpec((rows, D_MODEL), lambda l, j: (0, 0)),
            pl.BlockSpec((None, D_MODEL, ADA_TILE), lambda l, j: (l, 0, j)),
            pl.BlockSpec((None, 1, ADA_TILE), lambda l, j: (l, 0, j)),
        ],
        out_specs=pl.BlockSpec((None, rows, ADA_TILE), lambda l, j: (l, 0, j)),
        compiler_params=_params(("arbitrary", "arbitrary")),
        name="ada_proj",
    )(c_all, w_ada, b_ada.reshape(DEPTH, 1, n_out))


def _x_spec(st, buffers=None):
    mode = None if buffers is None else pl.Buffered(buffers)
    return pl.BlockSpec((st.tile, D_MODEL), lambda i, j: (i, 0), pipeline_mode=mode)


def _norm_spec(layer, sub):
    return pl.BlockSpec((None, 1, D_MODEL), lambda i, j: (layer * 3 + sub, 0, 0))


def _mod_spec(st, k, width=D_MODEL, by_col=False):
    tps = st.tiles_per_seq
    if by_col:
        return pl.BlockSpec((None, None, st.mod_rows, width), lambda i, j: (k, i // tps, 0, j))
    return pl.BlockSpec((None, None, st.mod_rows, width), lambda i, j: (k, i // tps, 0, 0))


def _ffn_kernel(x_ref, g_ref, sh_ref, sc_ref, gt_ref, wgu_ref, wd_ref, *rest, final):
    if final:
        fg_ref, o_ref, h_ref = rest
    else:
        o_ref, h_ref = rest
    j = pl.program_id(1)

    @pl.when(j == 0)
    def _():
        _norm_mod_rows(x_ref, g_ref, sh_ref, sc_ref, h_ref, o_ref)

    gu = _dot(h_ref[...], wgu_ref[...])
    a = (_silu(gu[:, :FF_TILE]) * gu[:, FF_TILE:]).astype(BF16)
    o_ref[...] += _dot(a, wd_ref[...]) * (0.5 * gt_ref[...])

    if final:
        @pl.when(j == pl.num_programs(1) - 1)
        def _():
            o_ref[...] = _rmsnorm(o_ref[...], fg_ref[...])


def _ffn_call(st, x, mods, norm_g, w_gu, w_d, layer, which, final_g=None):
    sub = 2 * which
    in_specs = [
        _x_spec(st, 1),
        _norm_spec(layer, sub),
        _mod_spec(st, 3 * sub), _mod_spec(st, 3 * sub + 1), _mod_spec(st, 3 * sub + 2),
        pl.BlockSpec((None, None, D_MODEL, 2 * FF_TILE), lambda i, j: (layer, which, 0, j)),
        pl.BlockSpec((None, None, FF_TILE, D_MODEL), lambda i, j: (layer, which, j, 0)),
    ]
    args = [x, norm_g, mods, mods, mods, w_gu, w_d]
    if final_g is not None:
        in_specs.append(pl.BlockSpec((1, D_MODEL), lambda i, j: (0, 0)))
        args.append(final_g.reshape(1, D_MODEL))
    return pl.pallas_call(
        functools.partial(_ffn_kernel, final=final_g is not None),
        out_shape=jax.ShapeDtypeStruct((st.rows, D_MODEL), F32),
        grid=(st.rows // st.tile, FF_PAD // FF_TILE),
        in_specs=in_specs,
        out_specs=_x_spec(st),
        scratch_shapes=[pltpu.VMEM((st.tile, D_MODEL), BF16)],
        compiler_params=_params(("arbitrary", "arbitrary")),
        name="ffn",
    )(*args)


def _conv_kernel(x_ref, g_ref, sh_ref, sc_ref, gt_ref, wb_ref, wc_ref, wv_ref, cw_ref, wo_ref,
                 *rest, seg_rows, tiles_per_seq):
    if seg_rows is None:
        o_ref, st_ref, h_ref, ubuf_ref, carry_ref = rest
    else:
        p1_ref, p2_ref, o_ref, st_ref, h_ref, ubuf_ref = rest
    i = pl.program_id(0)
    j = pl.program_id(1)
    tm = x_ref.shape[0]

    @pl.when(j == 0)
    def _():
        _norm_mod_rows(x_ref, g_ref, sh_ref, sc_ref, h_ref, o_ref)

    h = h_ref[...]
    y = None
    for c in range(wb_ref.shape[1] // MIX_CHUNK):
        cols = slice(c * MIX_CHUNK, (c + 1) * MIX_CHUNK)
        u = _dot(h, wc_ref[:, cols]) * _dot(h, wv_ref[:, cols])
        ubuf_ref[SUBLANES:, cols] = u
        if seg_rows is None:
            prev = carry_ref[j, :, cols]
            ubuf_ref[:SUBLANES, cols] = jnp.where(i % tiles_per_seq == 0, jnp.zeros_like(prev), prev)
            tail = u[tm - SUBLANES:, :]
            carry_ref[j, :, cols] = tail
            st_ref[:, cols] = tail
            prev1 = ubuf_ref[SUBLANES - 1:SUBLANES - 1 + tm, cols]
            prev2 = ubuf_ref[SUBLANES - 2:SUBLANES - 2 + tm, cols]
        else:
            ubuf_ref[:SUBLANES, cols] = jnp.zeros((SUBLANES, MIX_CHUNK), F32)
            st_ref[:, cols] = u
            t = lax.broadcasted_iota(jnp.int32, u.shape, 0) % seg_rows
            prev1 = jnp.where(t >= 1, ubuf_ref[SUBLANES - 1:SUBLANES - 1 + tm, cols], p1_ref[:, cols])
            prev2 = jnp.where(t >= 2, ubuf_ref[SUBLANES - 2:SUBLANES - 2 + tm, cols], p2_ref[:, cols])
        conv = cw_ref[0:1, cols] * prev2 + cw_ref[1:2, cols] * prev1 + cw_ref[2:3, cols] * u
        gated = (_dot(h, wb_ref[:, cols]) * conv).astype(BF16)
        part = _dot(gated, wo_ref[cols, :])
        y = part if y is None else y + part
    o_ref[...] += y * gt_ref[...]


def _conv_call(st, x, mods, norm_g, w_in, conv_w, w_out, layer, j_conv, prefix=None, seg_rows=None):
    if st.tile > MIX_ROWS:
        st = _Stream(st.rows, MIX_ROWS, st.tiles_per_seq * (st.tile // MIX_ROWS), st.mod_rows)
    tn = MIX_TILE
    nj = D_MODEL // tn
    n_tiles = st.rows // st.tile
    col = pl.BlockSpec((st.tile, tn), lambda i, j: (i, j))
    in_specs = [
        _x_spec(st, 1),
        _norm_spec(layer, 1),
        _mod_spec(st, 3), _mod_spec(st, 4), _mod_spec(st, 5),
        pl.BlockSpec((None, D_MODEL, tn), lambda i, j: (j_conv, 0, j)),
        pl.BlockSpec((None, D_MODEL, tn), lambda i, j: (j_conv, 0, nj + j)),
        pl.BlockSpec((None, D_MODEL, tn), lambda i, j: (j_conv, 0, 2 * nj + j)),
        pl.BlockSpec((None, CONV_WIDTH, tn), lambda i, j: (j_conv, 0, j)),
        pl.BlockSpec((None, tn, D_MODEL), lambda i, j: (j_conv, j, 0)),
    ]
    args = [x, norm_g, mods, mods, mods, w_in, w_in, w_in, conv_w, w_out]
    scratch = [pltpu.VMEM((st.tile, D_MODEL), BF16), pltpu.VMEM((st.tile + SUBLANES, tn), F32)]
    if prefix is None:
        st_shape = jax.ShapeDtypeStruct((n_tiles, SUBLANES, D_MODEL), F32)
        st_spec = pl.BlockSpec((None, SUBLANES, tn), lambda i, j: (i, 0, j))
        scratch.append(pltpu.VMEM((nj, SUBLANES, tn), F32))
    else:
        in_specs += [col, col]
        args += list(prefix)
        st_shape = jax.ShapeDtypeStruct((st.rows, D_MODEL), F32)
        st_spec = col
    return pl.pallas_call(
        functools.partial(_conv_kernel, seg_rows=seg_rows, tiles_per_seq=st.tiles_per_seq),
        out_shape=(jax.ShapeDtypeStruct((st.rows, D_MODEL), F32), st_shape),
        grid=(n_tiles, nj),
        in_specs=in_specs,
        out_specs=(_x_spec(st), st_spec),
        scratch_shapes=scratch,
        compiler_params=_params(("arbitrary", "arbitrary")),
        name="conv_mixer",
    )(*args)


def _rope_tile(x, cos, sin, first_half):
    outs = []
    for c in range(x.shape[1] // LANES):
        xc = x[:, c * LANES:(c + 1) * LANES]
        partner = jnp.where(first_half, pltpu.roll(xc, LANES - HEAD_DIM // 2, 1),
                            pltpu.roll(xc, HEAD_DIM // 2, 1))
        outs.append(xc * cos + partner * sin)
    return jnp.concatenate(outs, axis=1)


def _qkv_kernel(x_ref, g_ref, sh_ref, sc_ref, wq_ref, wk_ref, wv_ref, cos_ref, sin_ref,
                q_ref, k_ref, vt_ref, kt_ref, vf_ref, h_ref):
    @pl.when(pl.program_id(1) == 0)
    def _():
        _norm_mod_rows(x_ref, g_ref, sh_ref, sc_ref, h_ref)

    h = h_ref[...]
    cos = cos_ref[...]
    sin = sin_ref[...]
    first_half = lax.broadcasted_iota(jnp.int32, cos.shape, 1) % HEAD_DIM < HEAD_DIM // 2
    q = _rope_tile(_dot(h, wq_ref[...]), cos, sin, first_half)
    k = _rope_tile(_dot(h, wk_ref[...]), cos, sin, first_half)
    v = _dot(h, wv_ref[...])
    q_ref[...] = q
    k_ref[...] = k.astype(BF16)
    kt_ref[...] = k.T
    vt_ref[...] = v.T.astype(BF16)
    vf_ref[...] = v


def _qkv_call(st, x, mods, norm_g, w_qkv, cos_t, sin_t, layer, j_attn):
    tn = QKV_TILE
    nj = D_MODEL // tn
    tps = st.tiles_per_seq
    seq_rows = st.tile * tps
    n_seq = st.rows // seq_rows
    col = pl.BlockSpec((st.tile, tn), lambda i, j: (i, j))
    col_t = pl.BlockSpec((None, tn, st.tile), lambda i, j: (i // tps, j, i % tps))
    rope = pl.BlockSpec((st.tile, LANES), lambda i, j: (i % tps, 0))
    return pl.pallas_call(
        _qkv_kernel,
        out_shape=(
            jax.ShapeDtypeStruct((st.rows, D_MODEL), F32),
            jax.ShapeDtypeStruct((st.rows, D_MODEL), BF16),
            jax.ShapeDtypeStruct((n_seq, D_MODEL, seq_rows), BF16),
            jax.ShapeDtypeStruct((n_seq, D_MODEL, seq_rows), F32),
            jax.ShapeDtypeStruct((st.rows, D_MODEL), F32),
        ),
        grid=(st.rows // st.tile, nj),
        in_specs=[
            _x_spec(st, 1),
            _norm_spec(layer, 1),
            _mod_spec(st, 3), _mod_spec(st, 4),
            pl.BlockSpec((None, D_MODEL, tn), lambda i, j: (j_attn, 0, j)),
            pl.BlockSpec((None, D_MODEL, tn), lambda i, j: (j_attn, 0, nj + j)),
            pl.BlockSpec((None, D_MODEL, tn), lambda i, j: (j_attn, 0, 2 * nj + j)),
            rope, rope,
        ],
        out_specs=(col, col, col_t, col_t, col),
        scratch_shapes=[pltpu.VMEM((st.tile, D_MODEL), BF16)],
        compiler_params=_params(("arbitrary", "arbitrary")),
        name="qkv_rope",
    )(x, norm_g, mods, mods, w_qkv, w_qkv, w_qkv, cos_t, sin_t)


def _proj_kernel(a_ref, w_ref, x_ref, gt_ref, o_ref):
    o_ref[...] = x_ref[...] + gt_ref[...] * _dot(a_ref[...], w_ref[...])


def _proj_call(st, a, w_out, x, mods, j_attn):
    tn = PROJ_TILE
    col = pl.BlockSpec((st.tile, tn), lambda i, j: (i, j))
    return pl.pallas_call(
        _proj_kernel,
        out_shape=jax.ShapeDtypeStruct((st.rows, D_MODEL), F32),
        grid=(st.rows // st.tile, D_MODEL // tn),
        in_specs=[
            pl.BlockSpec((st.tile, D_MODEL), lambda i, j: (i, 0)),
            pl.BlockSpec((None, D_MODEL, tn), lambda i, j: (j_attn, 0, j)),
            col,
            _mod_spec(st, 5, width=tn, by_col=True),
        ],
        out_specs=col,
        compiler_params=_params(("arbitrary", "arbitrary")),
        name="out_proj",
    )(a, w_out, x, mods)


def _subln(o, g, lam_init):
    return o * lax.rsqrt(jnp.mean(o * o, axis=-1, keepdims=True) + EPS) * g * (1.0 - lam_init)


def _flash_kernel(lam_ref, q_ref, k_ref, vt_ref, sg_ref, o_ref, qt_ref, s_ref, m_ref, l_ref, acc_ref, *,
                  lam_init):
    t = ATTN_TILE
    scale = HEAD_DIM ** -0.5 * math.log2(math.e)

    def scores(ki):
        k0 = pl.multiple_of(ki * t, t)
        return _dot(k_ref[pl.ds(k0, t), :], qt_ref[...])

    def update(st, ki, diagonal):
        if diagonal:
            key = lax.broadcasted_iota(jnp.int32, st.shape, 0)
            qry = lax.broadcasted_iota(jnp.int32, st.shape, 1) % t
            st = jnp.where(key <= qry, st, NEG_INF)
        k0 = pl.multiple_of(ki * t, t)
        m_prev = m_ref[...]
        m_new = jnp.maximum(m_prev, jnp.max(st, axis=0, keepdims=True))
        alpha = jnp.exp2(m_prev - m_new)
        p = jnp.exp2(st - m_new)
        l_ref[...] = alpha * l_ref[...] + jnp.sum(p, axis=0, keepdims=True)
        acc_ref[...] = alpha * acc_ref[...] + _dot(vt_ref[:, pl.ds(k0, t)], p.astype(BF16))
        m_ref[...] = m_new

    def off_diagonal(ki, carry):
        nxt = scores(ki + 1)
        update(s_ref[ki % 2], ki, False)
        s_ref[(ki + 1) % 2] = nxt
        return carry

    def q_block(qi, carry):
        q0 = pl.multiple_of(qi * t, t)
        q = q_ref[pl.ds(q0, t), :].astype(F32) * scale
        lane = lax.broadcasted_iota(jnp.int32, q.shape, 1)
        qt_ref[:, :t] = jnp.where(lane < HEAD_DIM, q, 0.0).T.astype(BF16)
        qt_ref[:, t:] = jnp.where(lane >= HEAD_DIM, q, 0.0).T.astype(BF16)
        m_ref[...] = jnp.full(m_ref.shape, NEG_INF, F32)
        l_ref[...] = jnp.zeros(l_ref.shape, F32)
        acc_ref[...] = jnp.zeros(acc_ref.shape, F32)
        s_ref[0] = scores(0)
        lax.fori_loop(0, qi, off_diagonal, 0)
        update(s_ref[qi % 2], qi, True)
        ot = acc_ref[...] * (1.0 / l_ref[...])
        d = ot[:, :t] - lam_ref[0] * ot[:, t:]
        d = d * lax.rsqrt(jnp.mean(d * d, axis=0, keepdims=True) + EPS)
        o_ref[pl.ds(q0, t), :] = (d.T * sg_ref[...] * (1.0 - lam_init)).astype(BF16)
        return carry

    lax.fori_loop(0, q_ref.shape[0] // t, q_block, 0)


def _flash_call(q, k, vt, lam, subln_g, lam_init):
    n_seq, _, seq_len = vt.shape
    t = ATTN_TILE
    rows = pl.BlockSpec((seq_len, LANES), lambda b, h: (b, h))
    return pl.pallas_call(
        functools.partial(_flash_kernel, lam_init=lam_init),
        out_shape=jax.ShapeDtypeStruct(q.shape, BF16),
        grid=(n_seq, N_HEADS),
        in_specs=[
            pl.BlockSpec(memory_space=pltpu.SMEM),
            rows, rows,
            pl.BlockSpec((None, V_DIM, seq_len), lambda b, h: (b, h, 0)),
            pl.BlockSpec((1, V_DIM), lambda b, h: (0, 0)),
        ],
        out_specs=rows,
        scratch_shapes=[
            pltpu.VMEM((LANES, 2 * t), BF16), pltpu.VMEM((2, t, 2 * t), F32),
            pltpu.VMEM((1, 2 * t), F32), pltpu.VMEM((1, 2 * t), F32), pltpu.VMEM((V_DIM, 2 * t), F32),
        ],
        compiler_params=_params(("arbitrary", "arbitrary")),
        name="flash_diff_attn",
    )(lam, q, k, vt, subln_g)


def _paged_kernel(pt_ref, lam_ref, qbd_ref, kn_ref, vn_ref, e_ref, hm_ref, sg_ref, *rest,
                  n_pages, t_new, lam_init):
    pp = PAGES_PER_STEP
    k_refs = rest[:pp]
    v_refs = rest[pp:2 * pp]
    o_ref, s_ref, a_ref, acc_ref = rest[2 * pp:]
    step = pl.program_id(1)
    k_steps = n_pages // pp
    past = n_pages * PAGE_SIZE
    scale = HEAD_DIM ** -0.5
    n_rows = N_HEADS * t_new

    @pl.when(step < k_steps)
    def _():
        qbd = qbd_ref[...]
        for i in range(pp):
            col = pl.multiple_of((step * pp + i) * PAGE_SIZE, PAGE_SIZE)
            s_ref[:, pl.ds(col, PAGE_SIZE)] = _dot(qbd, k_refs[i][...].astype(BF16)) * scale

    @pl.when(step == k_steps - 1)
    def _():
        sn = lax.dot_general(qbd_ref[...], kn_ref[...], (((1,), (1,)), ((), ())),
                             preferred_element_type=F32) * scale
        tq = lax.broadcasted_iota(jnp.int32, sn.shape, 0) % t_new
        tk = lax.broadcasted_iota(jnp.int32, sn.shape, 1)
        s_ref[:, past:] = jnp.where(tk <= tq, sn, NEG_INF)
        sc = s_ref[...]
        p = jnp.exp(sc - jnp.max(sc, axis=1, keepdims=True))
        p = p * (1.0 / jnp.sum(p, axis=1, keepdims=True))
        a_ref[...] = (p[:n_rows] - lam_ref[0] * p[n_rows:]).astype(BF16)
        acc_ref[...] = jnp.zeros(acc_ref.shape, F32)

    def weighted_values(a, values):
        spread = (_dot(a, e_ref[...]) * hm_ref[...]).astype(BF16)
        return _dot(spread[:, :values.shape[0]], values)

    @pl.when(step >= k_steps)
    def _():
        acc = acc_ref[...]
        for i in range(pp):
            col = pl.multiple_of(((step - k_steps) * pp + i) * PAGE_SIZE, PAGE_SIZE)
            acc += weighted_values(a_ref[:, pl.ds(col, PAGE_SIZE)], v_refs[i][...].astype(BF16))
        acc_ref[...] = acc

    @pl.when(step == 2 * k_steps - 1)
    def _():
        o = acc_ref[...] + weighted_values(a_ref[:, past:], vn_ref[...])
        o_ref[...] = _subln(o, sg_ref[...], lam_init).astype(BF16)


def _paged_call(page_table, lam, qbd, k_new, v_new, spread, head_mask, subln_g, k_pages, v_pages, lam_init):
    n_seq, n_pages = page_table.shape
    pp = PAGES_PER_STEP
    k_steps = n_pages // pp
    n_rows = head_mask.shape[0]
    t_new = n_rows // N_HEADS
    keys = n_pages * PAGE_SIZE + LANES
    page_rows = k_pages.shape[1]

    def k_spec(i):
        return pl.BlockSpec(
            (None, page_rows, PAGE_SIZE),
            lambda b, s, pt: (pt[b * n_pages + jnp.minimum(s, k_steps - 1) * pp + i], 0, 0))

    def v_spec(i):
        return pl.BlockSpec(
            (None, page_rows, V_DIM),
            lambda b, s, pt: (pt[b * n_pages + jnp.maximum(s - k_steps, 0) * pp + i], 0, 0))

    def whole(shape):
        return pl.BlockSpec(shape, lambda b, s, pt: (0,) * len(shape))

    def per_seq(shape):
        return pl.BlockSpec((None,) + shape, lambda b, s, pt: (b,) + (0,) * len(shape))

    grid_spec = pltpu.PrefetchScalarGridSpec(
        num_scalar_prefetch=1,
        grid=(n_seq, 2 * k_steps),
        in_specs=[
            pl.BlockSpec(memory_space=pltpu.SMEM),
            per_seq((2 * n_rows, D_MODEL)),
            per_seq((LANES, D_MODEL)),
            per_seq((LANES, V_DIM)),
            whole(spread.shape), whole(head_mask.shape), whole((1, V_DIM)),
        ] + [k_spec(i) for i in range(pp)] + [v_spec(i) for i in range(pp)],
        out_specs=per_seq((n_rows, V_DIM)),
        scratch_shapes=[
            pltpu.VMEM((2 * n_rows, keys), F32), pltpu.VMEM((n_rows, keys), BF16),
            pltpu.VMEM((n_rows, V_DIM), F32),
        ],
    )
    return pl.pallas_call(
        functools.partial(_paged_kernel, n_pages=n_pages, t_new=t_new, lam_init=lam_init),
        out_shape=jax.ShapeDtypeStruct((n_seq, n_rows, V_DIM), BF16),
        grid_spec=grid_spec,
        compiler_params=_params(("arbitrary", "arbitrary")),
        name="paged_diff_attn",
    )(page_table.reshape(-1), lam, qbd, k_new, v_new, spread, head_mask, subln_g,
      *([k_pages] * pp), *([v_pages] * pp))


def _rope_tables(pos):
    half = HEAD_DIM // 2
    inv = ROPE_THETA ** (-jnp.arange(half, dtype=F32) / half)
    ang = pos.astype(F32)[:, None] * inv[None, :]
    cos = jnp.cos(ang)
    sin = jnp.sin(ang)
    reps = LANES // HEAD_DIM
    return (jnp.tile(jnp.concatenate([cos, cos], axis=1), (1, reps)),
            jnp.tile(jnp.concatenate([-sin, sin], axis=1), (1, reps)))


def _sample_attention(q, k, v, page_table, lam, subln_g, k_pages, v_pages, lam_init):
    n_seq = page_table.shape[0]
    t_new = q.shape[0] // n_seq
    n_sub = 2 * N_HEADS
    head_of_lane = jnp.arange(D_MODEL) // HEAD_DIM
    sub_head = 2 * jnp.arange(N_HEADS)[None, :] + jnp.arange(2)[:, None]
    keep = (head_of_lane[None, None, :] == sub_head[:, :, None]).astype(BF16)
    q3 = q.astype(BF16).reshape(n_seq, 1, 1, t_new, D_MODEL)
    qbd = (q3 * keep[None, :, :, None, :]).reshape(n_seq, n_sub * t_new, D_MODEL)
    k_new = jnp.pad(k.reshape(n_seq, t_new, D_MODEL), ((0, 0), (0, LANES - t_new), (0, 0)))
    v_new = jnp.pad(v.reshape(n_seq, t_new * N_HEADS, V_DIM), ((0, 0), (0, LANES - t_new * N_HEADS), (0, 0)))
    lane = jnp.arange(PAGE_SIZE * N_HEADS)
    spread = (lane[None, :] // N_HEADS == jnp.arange(PAGE_SIZE)[:, None]).astype(BF16)
    row_head = jnp.arange(N_HEADS * t_new) // t_new
    head_mask = (lane[None, :] % N_HEADS == row_head[:, None]).astype(F32)
    o = _paged_call(page_table, lam, qbd, k_new, v_new, spread, head_mask, subln_g,
                    k_pages, v_pages, lam_init)
    o = o.reshape(n_seq, N_HEADS, t_new, V_DIM).transpose(0, 2, 1, 3)
    return o.reshape(n_seq * t_new, D_MODEL)


def _trunk(st, x, mods, pos_tables, p, conv_prefix, seg_rows, attend):
    conv_state = kt = vf = None
    cos_t, sin_t = pos_tables
    for layer in range(DEPTH):
        j = layer // 2
        m = mods[layer]
        x = _ffn_call(st, x, m, p['norm_g'], p['w_gu'], p['w_d'], layer, 0)
        if layer % 2 == 0:
            x, conv_state = _conv_call(st, x, m, p['norm_g'], p['conv_w_in'], p['conv_w'], p['conv_w_out'],
                                       layer, j, conv_prefix, seg_rows)
        else:
            q, k, vt, kt, vf = _qkv_call(st, x, m, p['norm_g'], p['attn_w_qkv'], cos_t, sin_t, layer, j)
            o = attend(q, k, vt, vf, _lambda_init(layer))
            x = _proj_call(st, o, p['attn_w_out'], x, m, j)
        final_g = p['final_g'] if layer == DEPTH - 1 else None
        x = _ffn_call(st, x, m, p['norm_g'], p['w_gu'], p['w_d'], layer, 1, final_g)
    return x, conv_state, kt, vf


def _k_from_transposed(kt, n_seq, rows_per_seq):
    s, _, rows = kt.shape
    k = kt.reshape(s, 2 * N_HEADS, HEAD_DIM, rows).transpose(0, 3, 1, 2)
    return k.reshape(1, n_seq, rows_per_seq, 2 * N_HEADS, HEAD_DIM)


def kernel(x_prompt, x_sample, c_prompt, c_sample, state_conv, cache_k, cache_v, page_table, norm_g, final_g,
           w_ada, b_ada, ffn_w_gate, ffn_w_up, ffn_w_down, conv_w_in, conv_w, conv_w_out, attn_w_qkv,
           attn_w_out, lambda_q1, lambda_k1, lambda_q2, lambda_k2, subln_g):
    n_prompt, seq_len, _ = x_prompt.shape
    n_sample, t_new, _ = x_sample.shape
    n_pages = page_table.shape[1]
    assert DEPTH == 2 and t_new >= CONV_WIDTH - 1 and seq_len % PROMPT_TILE == 0
    w_gu, w_d = _cast_ffn_weights(ffn_w_gate, ffn_w_up, ffn_w_down)
    p = dict(
        norm_g=norm_g.reshape(DEPTH * 3, 1, D_MODEL),
        final_g=final_g,
        w_gu=w_gu,
        w_d=w_d,
        conv_w_in=conv_w_in.astype(BF16),
        conv_w=conv_w,
        conv_w_out=conv_w_out.astype(BF16),
        attn_w_qkv=attn_w_qkv.astype(BF16),
        attn_w_out=attn_w_out.astype(BF16),
    )
    j_attn = 0
    lam = (jnp.exp(jnp.sum(lambda_q1[j_attn] * lambda_k1[j_attn]))
           - jnp.exp(jnp.sum(lambda_q2[j_attn] * lambda_k2[j_attn])) + _lambda_init(1)).reshape(1).astype(F32)
    sg = subln_g[j_attn].reshape(1, V_DIM)

    n_cond = n_prompt + n_sample
    c_rows = ((n_cond + SUBLANES - 1) // SUBLANES) * SUBLANES
    c_all = jnp.pad(jnp.concatenate([c_prompt, c_sample], axis=0), ((0, c_rows - n_cond), (0, 0)))
    mods = _ada_call(c_all, w_ada, b_ada).reshape(DEPTH, c_rows, N_ADA, D_MODEL)
    mods_p = [mods[l, :n_prompt].transpose(1, 0, 2)[:, :, None, :] for l in range(DEPTH)]
    mods_s = [jnp.repeat(mods[l, n_prompt:n_cond], t_new, axis=0).transpose(1, 0, 2)[:, None]
              for l in range(DEPTH)]

    tiles_per_seq = seq_len // PROMPT_TILE
    st_p = _Stream(n_prompt * seq_len, PROMPT_TILE, tiles_per_seq, 1)
    pos_p = _rope_tables(jnp.arange(seq_len, dtype=jnp.int32))
    attend_p = lambda q, k, vt, vf, li: _flash_call(q, k, vt, lam, sg, li)
    y_p, cs_p, kt_p, v_p = _trunk(st_p, x_prompt.reshape(-1, D_MODEL), mods_p, pos_p, p, None, None, attend_p)

    rows_s = n_sample * t_new
    st_s = _Stream(rows_s, rows_s, 1, rows_s)
    pos_s = _rope_tables(n_pages * PAGE_SIZE + jnp.arange(rows_s, dtype=jnp.int32) % t_new)
    t_idx = (jnp.arange(rows_s) % t_new)[:, None]
    pre0 = jnp.repeat(state_conv[0, :, 0], t_new, axis=0)
    pre1 = jnp.repeat(state_conv[0, :, 1], t_new, axis=0)
    prefix1 = jnp.where(t_idx == 0, pre1, 0.0)
    prefix2 = jnp.where(t_idx == 0, pre0, jnp.where(t_idx == 1, pre1, 0.0))
    k_pages = cache_k[j_attn].transpose(0, 2, 3, 1).reshape(-1, 2 * N_HEADS * HEAD_DIM, PAGE_SIZE)
    v_pages = cache_v[j_attn].reshape(-1, PAGE_SIZE * N_HEADS, V_DIM)
    attend_s = lambda q, k, vt, vf, li: _sample_attention(q, k, vf.astype(BF16), page_table, lam, sg,
                                                          k_pages, v_pages, li)
    y_s, u_s, kt_s, v_s = _trunk(st_s, x_sample.reshape(-1, D_MODEL), mods_s, pos_s, p,
                                 (prefix1, prefix2), t_new, attend_s)

    keep = CONV_WIDTH - 1
    cs_p = cs_p.reshape(n_prompt, -1, SUBLANES, D_MODEL)[:, -1, SUBLANES - keep:, :]
    return (
        y_p.reshape(n_prompt, seq_len, D_MODEL),
        y_s.reshape(n_sample, t_new, D_MODEL),
        cs_p[None],
        u_s.reshape(n_sample, t_new, D_MODEL)[None, :, t_new - keep:, :],
        _k_from_transposed(kt_p, n_prompt, seq_len),
        v_p.reshape(1, n_prompt, seq_len, N_HEADS, V_DIM),
        _k_from_transposed(kt_s, n_sample, t_new),
        v_s.reshape(1, n_sample, t_new, N_HEADS, V_DIM),
    )
```

```python
import functools
import math

import jax
import jax.numpy as jnp
from jax import lax
from jax.experimental import pallas as pl
from jax.experimental.pallas import tpu as pltpu

F32 = jnp.float32
BF16 = jnp.bfloat16

D_MODEL = 2048
DEPTH = 2
N_HEADS = 16
HEAD_DIM = 64
V_DIM = 128
D_FF = 5504
N_ADA = 9
CONV_WIDTH = 3
ROPE_THETA = 10000.0
EPS = 1e-5
NEG_INF = -1e30
PAGE_SIZE = 128

LANES = 128
SUBLANES = 8
FF_TILE = 512
FF_PAD = ((D_FF + FF_TILE - 1) // FF_TILE) * FF_TILE
PROMPT_TILE = 1024
MIX_ROWS = 512
MIX_TILE = 512
MIX_CHUNK = 256
QKV_TILE = 512
PROJ_TILE = 1024
ADA_TILE = 1024
CAST_TILE = 256
ATTN_TILE = 512
PAGES_PER_STEP = 4
VMEM_LIMIT = 60000 * 1024


def _lambda_init(layer_idx):
    return 0.8 - 0.6 * math.exp(-0.3 * layer_idx)


class _Stream:
    def __init__(self, rows, tile, tiles_per_seq, mod_rows):
        self.rows = rows
        self.tile = tile
        self.tiles_per_seq = tiles_per_seq
        self.mod_rows = mod_rows


def _params(sem):
    return pltpu.CompilerParams(dimension_semantics=sem, vmem_limit_bytes=VMEM_LIMIT)


def _silu(x):
    return x * jax.nn.sigmoid(x)


def _rmsnorm(x, g):
    return x * lax.rsqrt(jnp.mean(x * x, axis=-1, keepdims=True) + EPS) * g


def _dot(a, b):
    return jnp.dot(a, b, preferred_element_type=F32)


def _norm_mod_rows(x_ref, g_ref, sh_ref, sc_ref, h_ref, copy_ref=None):
    x = x_ref[...]
    h_ref[...] = (_rmsnorm(x, g_ref[...]) * (1 + sc_ref[...]) + sh_ref[...]).astype(BF16)
    if copy_ref is not None:
        copy_ref[...] = x


def _cast_gate_up_kernel(g_ref, u_ref, o_ref):
    rows = o_ref.shape[0]
    for c in range(FF_PAD // FF_TILE):
        lo = c * FF_TILE
        n = min(FF_TILE, D_FF - lo)
        for k, ref in enumerate((g_ref, u_ref)):
            dst = 2 * lo + k * FF_TILE
            o_ref[:, dst:dst + n] = ref[:, lo:lo + n].astype(BF16)
            if n < FF_TILE:
                o_ref[:, dst + n:dst + FF_TILE] = jnp.zeros((rows, FF_TILE - n), BF16)


def _cast_down_kernel(w_ref, o_ref):
    o_ref[:D_FF, :] = w_ref[...].astype(BF16)
    o_ref[D_FF:, :] = jnp.zeros((FF_PAD - D_FF, o_ref.shape[1]), BF16)


def _cast_ffn_weights(w_gate, w_up, w_down):
    n_l, n_w = w_gate.shape[:2]
    t = CAST_TILE
    cp = _params(("arbitrary", "arbitrary", "arbitrary"))
    rows_in = pl.BlockSpec((None, None, t, D_FF), lambda l, w, i: (l, w, i, 0))
    w_gu = pl.pallas_call(
        _cast_gate_up_kernel,
        out_shape=jax.ShapeDtypeStruct((n_l, n_w, D_MODEL, 2 * FF_PAD), BF16),
        grid=(n_l, n_w, D_MODEL // t),
        in_specs=[rows_in, rows_in],
        out_specs=pl.BlockSpec((None, None, t, 2 * FF_PAD), lambda l, w, i: (l, w, i, 0)),
        compiler_params=cp,
        name="cast_gate_up",
    )(w_gate, w_up)
    w_d = pl.pallas_call(
        _cast_down_kernel,
        out_shape=jax.ShapeDtypeStruct((n_l, n_w, FF_PAD, D_MODEL), BF16),
        grid=(n_l, n_w, D_MODEL // t),
        in_specs=[pl.BlockSpec((None, None, D_FF, t), lambda l, w, i: (l, w, 0, i))],
        out_specs=pl.BlockSpec((None, None, FF_PAD, t), lambda l, w, i: (l, w, 0, i)),
        compiler_params=cp,
        name="cast_down",
    )(w_down)
    return w_gu, w_d


def _ada_kernel(c_ref, w_ref, b_ref, o_ref):
    a = _silu(c_ref[...]).astype(BF16)
    o_ref[...] = _dot(a, w_ref[...].astype(BF16)) + b_ref[...]


def _ada_call(c_all, w_ada, b_ada):
    rows = c_all.shape[0]
    n_out = N_ADA * D_MODEL
    return pl.pallas_call(
        _ada_kernel,
        out_shape=jax.ShapeDtypeStruct((DEPTH, rows, n_out), F32),
        grid=(DEPTH, n_out // ADA_TILE),
        in_specs=[
            pl.BlockSpec((rows, D_MODEL), lambda l, j: (0, 0)),
            pl.BlockSpec((None, D_MODEL, ADA_TILE), lambda l, j: (l, 0, j)),
            pl.BlockSpec((None, 1, ADA_TILE), lambda l, j: (l, 0, j)),
        ],
        out_specs=pl.BlockSpec((None, rows, ADA_TILE), lambda l, j: (l, 0, j)),
        compiler_params=_params(("arbitrary", "arbitrary")),
        name="ada_proj",
    )(c_all, w_ada, b_ada.reshape(DEPTH, 1, n_out))


def _x_spec(st, buffers=None):
    mode = None if buffers is None else pl.Buffered(buffers)
    return pl.BlockSpec((st.tile, D_MODEL), lambda i, j: (i, 0), pipeline_mode=mode)


def _norm_spec(layer, sub):
    return pl.BlockSpec((None, 1, D_MODEL), lambda i, j: (layer * 3 + sub, 0, 0))


def _mod_spec(st, k, width=D_MODEL, by_col=False):
    tps = st.tiles_per_seq
    if by_col:
        return pl.BlockSpec((None, None, st.mod_rows, width), lambda i, j: (k, i // tps, 0, j))
    return pl.BlockSpec((None, None, st.mod_rows, width), lambda i, j: (k, i // tps, 0, 0))


def _ffn_kernel(x_ref, g_ref, sh_ref, sc_ref, gt_ref, wgu_ref, wd_ref, *rest, final):
    if final:
        fg_ref, o_ref, h_ref = rest
    else:
        o_ref, h_ref = rest
    j = pl.program_id(1)

    @pl.when(j == 0)
    def _():
        _norm_mod_rows(x_ref, g_ref, sh_ref, sc_ref, h_ref, o_ref)

    gu = _dot(h_ref[...], wgu_ref[...])
    a = (_silu(gu[:, :FF_TILE]) * gu[:, FF_TILE:]).astype(BF16)
    o_ref[...] += _dot(a, wd_ref[...]) * (0.5 * gt_ref[...])

    if final:
        @pl.when(j == pl.num_programs(1) - 1)
        def _():
            o_ref[...] = _rmsnorm(o_ref[...], fg_ref[...])


def _ffn_call(st, x, mods, norm_g, w_gu, w_d, layer, which, final_g=None):
    sub = 2 * which
    in_specs = [
        _x_spec(st, 1),
        _norm_spec(layer, sub),
        _mod_spec(st, 3 * sub), _mod_spec(st, 3 * sub + 1), _mod_spec(st, 3 * sub + 2),
        pl.BlockSpec((None, None, D_MODEL, 2 * FF_TILE), lambda i, j: (layer, which, 0, j)),
        pl.BlockSpec((None, None, FF_TILE, D_MODEL), lambda i, j: (layer, which, j, 0)),
    ]
    args = [x, norm_g, mods, mods, mods, w_gu, w_d]
    if final_g is not None:
        in_specs.append(pl.BlockSpec((1, D_MODEL), lambda i, j: (0, 0)))
        args.append(final_g.reshape(1, D_MODEL))
    return pl.pallas_call(
        functools.partial(_ffn_kernel, final=final_g is not None),
        out_shape=jax.ShapeDtypeStruct((st.rows, D_MODEL), F32),
        grid=(st.rows // st.tile, FF_PAD // FF_TILE),
        in_specs=in_specs,
        out_specs=_x_spec(st),
        scratch_shapes=[pltpu.VMEM((st.tile, D_MODEL), BF16)],
        compiler_params=_params(("arbitrary", "arbitrary")),
        name="ffn",
    )(*args)


def _conv_kernel(x_ref, g_ref, sh_ref, sc_ref, gt_ref, wb_ref, wc_ref, wv_ref, cw_ref, wo_ref,
                 *rest, seg_rows, tiles_per_seq):
    if seg_rows is None:
        o_ref, st_ref, h_ref, ubuf_ref, carry_ref = rest
    else:
        p1_ref, p2_ref, o_ref, st_ref, h_ref, ubuf_ref = rest
    i = pl.program_id(0)
    j = pl.program_id(1)
    tm = x_ref.shape[0]

    @pl.when(j == 0)
    def _():
        _norm_mod_rows(x_ref, g_ref, sh_ref, sc_ref, h_ref, o_ref)

    h = h_ref[...]
    y = None
    for c in range(wb_ref.shape[1] // MIX_CHUNK):
        cols = slice(c * MIX_CHUNK, (c + 1) * MIX_CHUNK)
        u = _dot(h, wc_ref[:, cols]) * _dot(h, wv_ref[:, cols])
        ubuf_ref[SUBLANES:, cols] = u
        if seg_rows is None:
            prev = carry_ref[j, :, cols]
            ubuf_ref[:SUBLANES, cols] = jnp.where(i % tiles_per_seq == 0, jnp.zeros_like(prev), prev)
            tail = u[tm - SUBLANES:, :]
            carry_ref[j, :, cols] = tail
            st_ref[:, cols] = tail
            prev1 = ubuf_ref[SUBLANES - 1:SUBLANES - 1 + tm, cols]
            prev2 = ubuf_ref[SUBLANES - 2:SUBLANES - 2 + tm, cols]
        else:
            ubuf_ref[:SUBLANES, cols] = jnp.zeros((SUBLANES, MIX_CHUNK), F32)
            st_ref[:, cols] = u
            t = lax.broadcasted_iota(jnp.int32, u.shape, 0) % seg_rows
            prev1 = jnp.where(t >= 1, ubuf_ref[SUBLANES - 1:SUBLANES - 1 + tm, cols], p1_ref[:, cols])
            prev2 = jnp.where(t >= 2, ubuf_ref[SUBLANES - 2:SUBLANES - 2 + tm, cols], p2_ref[:, cols])
        conv = cw_ref[0:1, cols] * prev2 + cw_ref[1:2, cols] * prev1 + cw_ref[2:3, cols] * u
        gated = (_dot(h, wb_ref[:, cols]) * conv).astype(BF16)
        part = _dot(gated, wo_ref[cols, :])
        y = part if y is None else y + part
    o_ref[...] += y * gt_ref[...]


def _conv_call(st, x, mods, norm_g, w_in, conv_w, w_out, layer, j_conv, prefix=None, seg_rows=None):
    if st.tile > MIX_ROWS:
        st = _Stream(st.rows, MIX_ROWS, st.tiles_per_seq * (st.tile // MIX_ROWS), st.mod_rows)
    tn = MIX_TILE
    nj = D_MODEL // tn
    n_tiles = st.rows // st.tile
    col = pl.BlockSpec((st.tile, tn), lambda i, j: (i, j))
    in_specs = [
        _x_spec(st, 1),
        _norm_spec(layer, 1),
        _mod_spec(st, 3), _mod_spec(st, 4), _mod_spec(st, 5),
        pl.BlockSpec((None, D_MODEL, tn), lambda i, j: (j_conv, 0, j)),
        pl.BlockSpec((None, D_MODEL, tn), lambda i, j: (j_conv, 0, nj + j)),
        pl.BlockSpec((None, D_MODEL, tn), lambda i, j: (j_conv, 0, 2 * nj + j)),
        pl.BlockSpec((None, CONV_WIDTH, tn), lambda i, j: (j_conv, 0, j)),
        pl.BlockSpec((None, tn, D_MODEL), lambda i, j: (j_conv, j, 0)),
    ]
    args = [x, norm_g, mods, mods, mods, w_in, w_in, w_in, conv_w, w_out]
    scratch = [pltpu.VMEM((st.tile, D_MODEL), BF16), pltpu.VMEM((st.tile + SUBLANES, tn), F32)]
    if prefix is None:
        st_shape = jax.ShapeDtypeStruct((n_tiles, SUBLANES, D_MODEL), F32)
        st_spec = pl.BlockSpec((None, SUBLANES, tn), lambda i, j: (i, 0, j))
        scratch.append(pltpu.VMEM((nj, SUBLANES, tn), F32))
    else:
        in_specs += [col, col]
        args += list(prefix)
        st_shape = jax.ShapeDtypeStruct((st.rows, D_MODEL), F32)
        st_spec = col
    return pl.pallas_call(
        functools.partial(_conv_kernel, seg_rows=seg_rows, tiles_per_seq=st.tiles_per_seq),
        out_shape=(jax.ShapeDtypeStruct((st.rows, D_MODEL), F32), st_shape),
        grid=(n_tiles, nj),
        in_specs=in_specs,
        out_specs=(_x_spec(st), st_spec),
        scratch_shapes=scratch,
        compiler_params=_params(("arbitrary", "arbitrary")),
        name="conv_mixer",
    )(*args)


def _rope_tile(x, cos, sin, first_half):
    outs = []
    for c in range(x.shape[1] // LANES):
        xc = x[:, c * LANES:(c + 1) * LANES]
        partner = jnp.where(first_half, pltpu.roll(xc, LANES - HEAD_DIM // 2, 1),
                            pltpu.roll(xc, HEAD_DIM // 2, 1))
        outs.append(xc * cos + partner * sin)
    return jnp.concatenate(outs, axis=1)


def _qkv_kernel(x_ref, g_ref, sh_ref, sc_ref, wq_ref, wk_ref, wv_ref, cos_ref, sin_ref,
                q_ref, k_ref, vt_ref, kt_ref, vf_ref, h_ref):
    @pl.when(pl.program_id(1) == 0)
    def _():
        _norm_mod_rows(x_ref, g_ref, sh_ref, sc_ref, h_ref)

    h = h_ref[...]
    cos = cos_ref[...]
    sin = sin_ref[...]
    first_half = lax.broadcasted_iota(jnp.int32, cos.shape, 1) % HEAD_DIM < HEAD_DIM // 2
    q = _rope_tile(_dot(h, wq_ref[...]), cos, sin, first_half)
    k = _rope_tile(_dot(h, wk_ref[...]), cos, sin, first_half)
    v = _dot(h, wv_ref[...])
    q_ref[...] = q
    k_ref[...] = k.astype(BF16)
    kt_ref[...] = k.T
    vt_ref[...] = v.T.astype(BF16)
    vf_ref[...] = v


def _qkv_call(st, x, mods, norm_g, w_qkv, cos_t, sin_t, layer, j_attn):
    tn = QKV_TILE
    nj = D_MODEL // tn
    tps = st.tiles_per_seq
    seq_rows = st.tile * tps
    n_seq = st.rows // seq_rows
    col = pl.BlockSpec((st.tile, tn), lambda i, j: (i, j))
    col_t = pl.BlockSpec((None, tn, st.tile), lambda i, j: (i // tps, j, i % tps))
    rope = pl.BlockSpec((st.tile, LANES), lambda i, j: (i % tps, 0))
    return pl.pallas_call(
        _qkv_kernel,
        out_shape=(
            jax.ShapeDtypeStruct((st.rows, D_MODEL), F32),
            jax.ShapeDtypeStruct((st.rows, D_MODEL), BF16),
            jax.ShapeDtypeStruct((n_seq, D_MODEL, seq_rows), BF16),
            jax.ShapeDtypeStruct((n_seq, D_MODEL, seq_rows), F32),
            jax.ShapeDtypeStruct((st.rows, D_MODEL), F32),
        ),
        grid=(st.rows // st.tile, nj),
        in_specs=[
            _x_spec(st, 1),
            _norm_spec(layer, 1),
            _mod_spec(st, 3), _mod_spec(st, 4),
            pl.BlockSpec((None, D_MODEL, tn), lambda i, j: (j_attn, 0, j)),
            pl.BlockSpec((None, D_MODEL, tn), lambda i, j: (j_attn, 0, nj + j)),
            pl.BlockSpec((None, D_MODEL, tn), lambda i, j: (j_attn, 0, 2 * nj + j)),
            rope, rope,
        ],
        out_specs=(col, col, col_t, col_t, col),
        scratch_shapes=[pltpu.VMEM((st.tile, D_MODEL), BF16)],
        compiler_params=_params(("arbitrary", "arbitrary")),
        name="qkv_rope",
    )(x, norm_g, mods, mods, w_qkv, w_qkv, w_qkv, cos_t, sin_t)


def _proj_kernel(a_ref, w_ref, x_ref, gt_ref, o_ref):
    o_ref[...] = x_ref[...] + gt_ref[...] * _dot(a_ref[...], w_ref[...])


def _proj_call(st, a, w_out, x, mods, j_attn):
    tn = PROJ_TILE
    col = pl.BlockSpec((st.tile, tn), lambda i, j: (i, j))
    return pl.pallas_call(
        _proj_kernel,
        out_shape=jax.ShapeDtypeStruct((st.rows, D_MODEL), F32),
        grid=(st.rows // st.tile, D_MODEL // tn),
        in_specs=[
            pl.BlockSpec((st.tile, D_MODEL), lambda i, j: (i, 0)),
            pl.BlockSpec((None, D_MODEL, tn), lambda i, j: (j_attn, 0, j)),
            col,
            _mod_spec(st, 5, width=tn, by_col=True),
        ],
        out_specs=col,
        compiler_params=_params(("arbitrary", "arbitrary")),
        name="out_proj",
    )(a, w_out, x, mods)


def _subln(o, g, lam_init):
    return o * lax.rsqrt(jnp.mean(o * o, axis=-1, keepdims=True) + EPS) * g * (1.0 - lam_init)


def _flash_part(part, n_parts, lam_ref, q_ref, k_ref, vt_ref, sg_ref, o_ref, qt_ref, s_ref, m_ref, l_ref,
                acc_ref, lam_init):
    t = ATTN_TILE
    per_block = n_parts // (q_ref.shape[0] // t)
    qi = part // per_block
    sub = part % per_block
    scale = HEAD_DIM ** -0.5 * math.log2(math.e)

    def scores(ki):
        k0 = pl.multiple_of(ki * t, t)
        return _dot(k_ref[pl.ds(k0, t), :], qt_ref[...])

    def update(st, ki, diagonal):
        if diagonal:
            key = lax.broadcasted_iota(jnp.int32, st.shape, 0)
            qry = lax.broadcasted_iota(jnp.int32, st.shape, 1) % t
            st = jnp.where(key <= qry, st, NEG_INF)
        k0 = pl.multiple_of(ki * t, t)
        m_prev = m_ref[...]
        m_new = jnp.maximum(m_prev, jnp.max(st, axis=0, keepdims=True))
        alpha = jnp.exp2(m_prev - m_new)
        p = jnp.exp2(st - m_new)
        l_ref[...] = alpha * l_ref[...] + jnp.sum(p, axis=0, keepdims=True)
        acc_ref[...] = alpha * acc_ref[...] + _dot(vt_ref[:, pl.ds(k0, t)], p.astype(BF16))
        m_ref[...] = m_new

    def off_diagonal(ki, carry):
        nxt = scores(ki + 1)
        update(s_ref[ki % 2], ki, False)
        s_ref[(ki + 1) % 2] = nxt
        return carry

    q0 = pl.multiple_of(qi * t, t)

    @pl.when(sub == 0)
    def _():
        q = q_ref[pl.ds(q0, t), :] * scale
        lane = lax.broadcasted_iota(jnp.int32, q.shape, 1)
        qt_ref[:, :t] = jnp.where(lane < HEAD_DIM, q, 0.0).T.astype(BF16)
        qt_ref[:, t:] = jnp.where(lane >= HEAD_DIM, q, 0.0).T.astype(BF16)
        m_ref[...] = jnp.full(m_ref.shape, NEG_INF, F32)
        l_ref[...] = jnp.zeros(l_ref.shape, F32)
        acc_ref[...] = jnp.zeros(acc_ref.shape, F32)
        s_ref[0] = scores(0)

    lax.fori_loop(qi * sub // per_block, qi * (sub + 1) // per_block, off_diagonal, 0)

    @pl.when(sub == per_block - 1)
    def _():
        update(s_ref[qi % 2], qi, True)
        ot = acc_ref[...] * (1.0 / l_ref[...])
        d = ot[:, :t] - lam_ref[0] * ot[:, t:]
        d = d * lax.rsqrt(jnp.mean(d * d, axis=0, keepdims=True) + EPS)
        o_ref[pl.ds(q0, t), :] = (d.T * sg_ref[...] * (1.0 - lam_init)).astype(BF16)


def _attn_kernel(pt_ref, lam_ref, qbd_ref, kn_ref, vn_ref, e_ref, hm_ref, sg_ref, fq_ref, fk_ref, fvt_ref,
                 *rest, n_pages, t_new, lam_init):
    pp = PAGES_PER_STEP
    k_refs = rest[:pp]
    v_refs = rest[pp:2 * pp]
    o_ref, fo_ref, s_ref, a_ref, acc_ref, qt_ref, fs_ref, fm_ref, fl_ref, facc_ref = rest[2 * pp:]
    step = pl.program_id(1)
    _flash_part(step, 2 * (n_pages // pp), lam_ref, fq_ref, fk_ref, fvt_ref, sg_ref, fo_ref,
                qt_ref, fs_ref, fm_ref, fl_ref, facc_ref, lam_init)
    k_steps = n_pages // pp
    past = n_pages * PAGE_SIZE
    scale = HEAD_DIM ** -0.5
    n_rows = N_HEADS * t_new

    @pl.when(step < k_steps)
    def _():
        qbd = qbd_ref[...]
        for i in range(pp):
            col = pl.multiple_of((step * pp + i) * PAGE_SIZE, PAGE_SIZE)
            s_ref[:, pl.ds(col, PAGE_SIZE)] = _dot(qbd, k_refs[i][...].astype(BF16)) * scale

    @pl.when(step == k_steps - 1)
    def _():
        sn = lax.dot_general(qbd_ref[...], kn_ref[...], (((1,), (1,)), ((), ())),
                             preferred_element_type=F32) * scale
        tq = lax.broadcasted_iota(jnp.int32, sn.shape, 0) % t_new
        tk = lax.broadcasted_iota(jnp.int32, sn.shape, 1)
        s_ref[:, past:] = jnp.where(tk <= tq, sn, NEG_INF)
        sc = s_ref[...]
        p = jnp.exp(sc - jnp.max(sc, axis=1, keepdims=True))
        p = p * (1.0 / jnp.sum(p, axis=1, keepdims=True))
        a_ref[...] = (p[:n_rows] - lam_ref[0] * p[n_rows:]).astype(BF16)
        acc_ref[...] = jnp.zeros(acc_ref.shape, F32)

    def weighted_values(a, values):
        spread = (_dot(a, e_ref[...]) * hm_ref[...]).astype(BF16)
        return _dot(spread[:, :values.shape[0]], values)

    @pl.when(step >= k_steps)
    def _():
        acc = acc_ref[...]
        for i in range(pp):
            col = pl.multiple_of(((step - k_steps) * pp + i) * PAGE_SIZE, PAGE_SIZE)
            acc += weighted_values(a_ref[:, pl.ds(col, PAGE_SIZE)], v_refs[i][...].astype(BF16))
        acc_ref[...] = acc

    @pl.when(step == 2 * k_steps - 1)
    def _():
        o = acc_ref[...] + weighted_values(a_ref[:, past:], vn_ref[...])
        o_ref[...] = _subln(o, sg_ref[...], lam_init).astype(BF16)


def _attn_call(page_table, lam, qbd, k_new, v_new, spread, head_mask, subln_g, k_pages, v_pages,
               q_p, k_p, vt_p, lam_init):
    n_seq, n_pages = page_table.shape
    pp = PAGES_PER_STEP
    k_steps = n_pages // pp
    n_rows = head_mask.shape[0]
    t_new = n_rows // N_HEADS
    keys = n_pages * PAGE_SIZE + LANES
    page_rows = k_pages.shape[1]
    n_prompt, _, seq_len = vt_p.shape
    t = ATTN_TILE
    assert n_seq == n_prompt * N_HEADS and (2 * k_steps) % (seq_len // t) == 0

    def head_rows():
        return pl.BlockSpec((seq_len, LANES), lambda b, s, pt: (b // N_HEADS, b % N_HEADS),
                            pipeline_mode=pl.Buffered(1))

    def k_spec(i):
        return pl.BlockSpec(
            (None, page_rows, PAGE_SIZE),
            lambda b, s, pt: (pt[b * n_pages + jnp.minimum(s, k_steps - 1) * pp + i], 0, 0))

    def v_spec(i):
        return pl.BlockSpec(
            (None, page_rows, V_DIM),
            lambda b, s, pt: (pt[b * n_pages + jnp.maximum(s - k_steps, 0) * pp + i], 0, 0))

    def whole(shape):
        return pl.BlockSpec(shape, lambda b, s, pt: (0,) * len(shape))

    def per_seq(shape):
        return pl.BlockSpec((None,) + shape, lambda b, s, pt: (b,) + (0,) * len(shape))

    grid_spec = pltpu.PrefetchScalarGridSpec(
        num_scalar_prefetch=1,
        grid=(n_seq, 2 * k_steps),
        in_specs=[
            pl.BlockSpec(memory_space=pltpu.SMEM),
            per_seq((2 * n_rows, D_MODEL)),
            per_seq((LANES, D_MODEL)),
            per_seq((LANES, V_DIM)),
            whole(spread.shape), whole(head_mask.shape), whole((1, V_DIM)),
            head_rows(), head_rows(),
            pl.BlockSpec((None, V_DIM, seq_len), lambda b, s, pt: (b // N_HEADS, b % N_HEADS, 0),
                         pipeline_mode=pl.Buffered(1)),
        ] + [k_spec(i) for i in range(pp)] + [v_spec(i) for i in range(pp)],
        out_specs=(
            per_seq((n_rows, V_DIM)),
            pl.BlockSpec((seq_len, LANES), lambda b, s, pt: (b // N_HEADS, b % N_HEADS)),
        ),
        scratch_shapes=[
            pltpu.VMEM((2 * n_rows, keys), F32), pltpu.VMEM((n_rows, keys), BF16),
            pltpu.VMEM((n_rows, V_DIM), F32),
            pltpu.VMEM((LANES, 2 * t), BF16), pltpu.VMEM((2, t, 2 * t), F32),
            pltpu.VMEM((1, 2 * t), F32), pltpu.VMEM((1, 2 * t), F32), pltpu.VMEM((V_DIM, 2 * t), F32),
        ],
    )
    return pl.pallas_call(
        functools.partial(_attn_kernel, n_pages=n_pages, t_new=t_new, lam_init=lam_init),
        out_shape=(
            jax.ShapeDtypeStruct((n_seq, n_rows, V_DIM), BF16),
            jax.ShapeDtypeStruct(q_p.shape, BF16),
        ),
        grid_spec=grid_spec,
        compiler_params=_params(("arbitrary", "arbitrary")),
        name="diff_attn",
    )(page_table.reshape(-1), lam, qbd, k_new, v_new, spread, head_mask, subln_g, q_p, k_p, vt_p,
      *([k_pages] * pp), *([v_pages] * pp))


def _rope_tables(pos):
    half = HEAD_DIM // 2
    inv = ROPE_THETA ** (-jnp.arange(half, dtype=F32) / half)
    ang = pos.astype(F32)[:, None] * inv[None, :]
    cos = jnp.cos(ang)
    sin = jnp.sin(ang)
    reps = LANES // HEAD_DIM
    return (jnp.tile(jnp.concatenate([cos, cos], axis=1), (1, reps)),
            jnp.tile(jnp.concatenate([-sin, sin], axis=1), (1, reps)))


def _attention(q, k, v, q_p, k_p, vt_p, page_table, lam, subln_g, k_pages, v_pages, lam_init):
    n_seq = page_table.shape[0]
    t_new = q.shape[0] // n_seq
    n_sub = 2 * N_HEADS
    head_of_lane = jnp.arange(D_MODEL) // HEAD_DIM
    sub_head = 2 * jnp.arange(N_HEADS)[None, :] + jnp.arange(2)[:, None]
    keep = (head_of_lane[None, None, :] == sub_head[:, :, None]).astype(BF16)
    q3 = q.astype(BF16).reshape(n_seq, 1, 1, t_new, D_MODEL)
    qbd = (q3 * keep[None, :, :, None, :]).reshape(n_seq, n_sub * t_new, D_MODEL)
    k_new = jnp.pad(k.reshape(n_seq, t_new, D_MODEL), ((0, 0), (0, LANES - t_new), (0, 0)))
    v_new = jnp.pad(v.reshape(n_seq, t_new * N_HEADS, V_DIM), ((0, 0), (0, LANES - t_new * N_HEADS), (0, 0)))
    lane = jnp.arange(PAGE_SIZE * N_HEADS)
    spread = (lane[None, :] // N_HEADS == jnp.arange(PAGE_SIZE)[:, None]).astype(BF16)
    row_head = jnp.arange(N_HEADS * t_new) // t_new
    head_mask = (lane[None, :] % N_HEADS == row_head[:, None]).astype(F32)
    o, o_p = _attn_call(page_table, lam, qbd, k_new, v_new, spread, head_mask, subln_g,
                        k_pages, v_pages, q_p, k_p, vt_p, lam_init)
    o = o.reshape(n_seq, N_HEADS, t_new, V_DIM).transpose(0, 2, 1, 3)
    return o.reshape(n_seq * t_new, D_MODEL), o_p


def _to_attention(st, x, mods, pos_tables, p, conv_prefix, seg_rows):
    cos_t, sin_t = pos_tables
    x = _ffn_call(st, x, mods[0], p['norm_g'], p['w_gu'], p['w_d'], 0, 0)
    x, conv_state = _conv_call(st, x, mods[0], p['norm_g'], p['conv_w_in'], p['conv_w'], p['conv_w_out'],
                               0, 0, conv_prefix, seg_rows)
    x = _ffn_call(st, x, mods[0], p['norm_g'], p['w_gu'], p['w_d'], 0, 1)
    x = _ffn_call(st, x, mods[1], p['norm_g'], p['w_gu'], p['w_d'], 1, 0)
    return (x, conv_state) + _qkv_call(st, x, mods[1], p['norm_g'], p['attn_w_qkv'], cos_t, sin_t, 1, 0)


def _from_attention(st, x, o, mods, p):
    x = _proj_call(st, o, p['attn_w_out'], x, mods[1], 0)
    return _ffn_call(st, x, mods[1], p['norm_g'], p['w_gu'], p['w_d'], 1, 1, p['final_g'])


def _k_from_transposed(kt, n_seq, rows_per_seq):
    s, _, rows = kt.shape
    k = kt.reshape(s, 2 * N_HEADS, HEAD_DIM, rows).transpose(0, 3, 1, 2)
    return k.reshape(1, n_seq, rows_per_seq, 2 * N_HEADS, HEAD_DIM)


def kernel(x_prompt, x_sample, c_prompt, c_sample, state_conv, cache_k, cache_v, page_table, norm_g, final_g,
           w_ada, b_ada, ffn_w_gate, ffn_w_up, ffn_w_down, conv_w_in, conv_w, conv_w_out, attn_w_qkv,
           attn_w_out, lambda_q1, lambda_k1, lambda_q2, lambda_k2, subln_g):
    n_prompt, seq_len, _ = x_prompt.shape
    n_sample, t_new, _ = x_sample.shape
    n_pages = page_table.shape[1]
    assert DEPTH == 2 and t_new >= CONV_WIDTH - 1 and seq_len % PROMPT_TILE == 0
    w_gu, w_d = _cast_ffn_weights(ffn_w_gate, ffn_w_up, ffn_w_down)
    p = dict(
        norm_g=norm_g.reshape(DEPTH * 3, 1, D_MODEL),
        final_g=final_g,
        w_gu=w_gu,
        w_d=w_d,
        conv_w_in=conv_w_in.astype(BF16),
        conv_w=conv_w,
        conv_w_out=conv_w_out.astype(BF16),
        attn_w_qkv=attn_w_qkv.astype(BF16),
        attn_w_out=attn_w_out.astype(BF16),
    )
    j_attn = 0
    lam = (jnp.exp(jnp.sum(lambda_q1[j_attn] * lambda_k1[j_attn]))
           - jnp.exp(jnp.sum(lambda_q2[j_attn] * lambda_k2[j_attn])) + _lambda_init(1)).reshape(1).astype(F32)
    sg = subln_g[j_attn].reshape(1, V_DIM)

    n_cond = n_prompt + n_sample
    c_rows = ((n_cond + SUBLANES - 1) // SUBLANES) * SUBLANES
    c_all = jnp.pad(jnp.concatenate([c_prompt, c_sample], axis=0), ((0, c_rows - n_cond), (0, 0)))
    mods = _ada_call(c_all, w_ada, b_ada).reshape(DEPTH, c_rows, N_ADA, D_MODEL)
    mods_p = [mods[l, :n_prompt].transpose(1, 0, 2)[:, :, None, :] for l in range(DEPTH)]
    mods_s = [jnp.repeat(mods[l, n_prompt:n_cond], t_new, axis=0).transpose(1, 0, 2)[:, None]
              for l in range(DEPTH)]

    st_p = _Stream(n_prompt * seq_len, PROMPT_TILE, seq_len // PROMPT_TILE, 1)
    pos_p = _rope_tables(jnp.arange(seq_len, dtype=jnp.int32))
    x_p, cs_p, q_p, k_p, vt_p, kt_p, v_p = _to_attention(
        st_p, x_prompt.reshape(-1, D_MODEL), mods_p, pos_p, p, None, None)

    rows_s = n_sample * t_new
    st_s = _Stream(rows_s, rows_s, 1, rows_s)
    pos_s = _rope_tables(n_pages * PAGE_SIZE + jnp.arange(rows_s, dtype=jnp.int32) % t_new)
    t_idx = (jnp.arange(rows_s) % t_new)[:, None]
    pre0 = jnp.repeat(state_conv[0, :, 0], t_new, axis=0)
    pre1 = jnp.repeat(state_conv[0, :, 1], t_new, axis=0)
    prefix1 = jnp.where(t_idx == 0, pre1, 0.0)
    prefix2 = jnp.where(t_idx == 0, pre0, jnp.where(t_idx == 1, pre1, 0.0))
    k_pages = cache_k[j_attn].transpose(0, 2, 3, 1).reshape(-1, 2 * N_HEADS * HEAD_DIM, PAGE_SIZE)
    v_pages = cache_v[j_attn].reshape(-1, PAGE_SIZE * N_HEADS, V_DIM)
    x_s, u_s, q_s, k_s, _, kt_s, v_s = _to_attention(
        st_s, x_sample.reshape(-1, D_MODEL), mods_s, pos_s, p, (prefix1, prefix2), t_new)

    o_s, o_p = _attention(q_s, k_s, v_s.astype(BF16), q_p, k_p, vt_p, page_table, lam, sg,
                          k_pages, v_pages, _lambda_init(1))
    y_p = _from_attention(st_p, x_p, o_p, mods_p, p)
    y_s = _from_attention(st_s, x_s, o_s, mods_s, p)

    keep = CONV_WIDTH - 1
    cs_p = cs_p.reshape(n_prompt, -1, SUBLANES, D_MODEL)[:, -1, SUBLANES - keep:, :]
    return (
        y_p.reshape(n_prompt, seq_len, D_MODEL),
        y_s.reshape(n_sample, t_new, D_MODEL),
        cs_p[None],
        u_s.reshape(n_sample, t_new, D_MODEL)[None, :, t_new - keep:, :],
        _k_from_transposed(kt_p, n_prompt, seq_len),
        v_p.reshape(1, n_prompt, seq_len, N_HEADS, V_DIM),
        _k_from_transposed(kt_s, n_sample, t_new),
        v_s.reshape(1, n_sample, t_new, N_HEADS, V_DIM),
    )
```

```python
import functools
import math

import jax
import jax.numpy as jnp
from jax import lax
from jax.experimental import pallas as pl
from jax.experimental.pallas import tpu as pltpu

F32 = jnp.float32
BF16 = jnp.bfloat16

D_MODEL = 2048
DEPTH = 2
N_HEADS = 16
HEAD_DIM = 64
V_DIM = 128
D_FF = 5504
N_ADA = 9
CONV_WIDTH = 3
ROPE_THETA = 10000.0
EPS = 1e-5
NEG_INF = -1e30
PAGE_SIZE = 128

LANES = 128
SUBLANES = 8
FF_TILE = 512
FF_PAD = ((D_FF + FF_TILE - 1) // FF_TILE) * FF_TILE
PROMPT_TILE = 1024
NORM_ROWS = 32
NORM_UNROLL = 4
MIX_ROWS = 512
MIX_TILE = 512
MIX_CHUNK = 256
QKV_TILE = 512
PROJ_TILE = 1024
ADA_TILE = 1024
CAST_TILE = 256
ATTN_TILE = 512
PAGES_PER_STEP = 4
VMEM_LIMIT = 60000 * 1024


def _lambda_init(layer_idx):
    return 0.8 - 0.6 * math.exp(-0.3 * layer_idx)


class _Stream:
    def __init__(self, rows, tile, tiles_per_seq, mod_rows):
        self.rows = rows
        self.tile = tile
        self.tiles_per_seq = tiles_per_seq
        self.mod_rows = mod_rows


def _params(sem):
    return pltpu.CompilerParams(dimension_semantics=sem, vmem_limit_bytes=VMEM_LIMIT)


def _silu(x):
    return x * jax.nn.sigmoid(x)


def _rmsnorm(x, g):
    return x * lax.rsqrt(jnp.mean(x * x, axis=-1, keepdims=True) + EPS) * g


def _dot(a, b):
    return jnp.dot(a, b, preferred_element_type=F32)


def _norm_mod_rows(x_ref, g_ref, sh_ref, sc_ref, h_ref, copy_ref=None):
    tm, d = x_ref.shape
    per_row = sh_ref.shape[0] > 1
    col_tiles = [slice(c * LANES, (c + 1) * LANES) for c in range(d // LANES)]

    def body(r, carry):
        rows = pl.ds(pl.multiple_of(r * NORM_ROWS, NORM_ROWS), NORM_ROWS)
        ss = None
        for cols in col_tiles:
            xc = x_ref[rows, cols]
            ss = xc * xc if ss is None else ss + xc * xc
        rs = lax.rsqrt(jnp.sum(ss, axis=-1, keepdims=True) * (1.0 / d) + EPS)
        rs = jnp.broadcast_to(rs, (NORM_ROWS, LANES))
        for cols in col_tiles:
            xc = x_ref[rows, cols]
            shift = sh_ref[rows, cols] if per_row else sh_ref[:, cols]
            scale = sc_ref[rows, cols] if per_row else sc_ref[:, cols]
            gain = g_ref[:, cols] * (1 + scale)
            h_ref[rows, cols] = (xc * rs * gain + shift).astype(BF16)
            if copy_ref is not None:
                copy_ref[rows, cols] = xc
        return carry

    lax.fori_loop(0, tm // NORM_ROWS, body, 0, unroll=NORM_UNROLL)


def _cast_gate_up_kernel(g_ref, u_ref, o_ref):
    rows = o_ref.shape[0]
    for c in range(FF_PAD // FF_TILE):
        lo = c * FF_TILE
        n = min(FF_TILE, D_FF - lo)
        for k, ref in enumerate((g_ref, u_ref)):
            dst = 2 * lo + k * FF_TILE
            o_ref[:, dst:dst + n] = ref[:, lo:lo + n].astype(BF16)
            if n < FF_TILE:
                o_ref[:, dst + n:dst + FF_TILE] = jnp.zeros((rows, FF_TILE - n), BF16)


def _cast_down_kernel(w_ref, o_ref):
    o_ref[:D_FF, :] = w_ref[...].astype(BF16)
    o_ref[D_FF:, :] = jnp.zeros((FF_PAD - D_FF, o_ref.shape[1]), BF16)


def _cast_ffn_weights(w_gate, w_up, w_down):
    n_l, n_w = w_gate.shape[:2]
    t = CAST_TILE
    cp = _params(("arbitrary", "arbitrary", "arbitrary"))
    rows_in = pl.BlockSpec((None, None, t, D_FF), lambda l, w, i: (l, w, i, 0))
    w_gu = pl.pallas_call(
        _cast_gate_up_kernel,
        out_shape=jax.ShapeDtypeStruct((n_l, n_w, D_MODEL, 2 * FF_PAD), BF16),
        grid=(n_l, n_w, D_MODEL // t),
        in_specs=[rows_in, rows_in],
        out_specs=pl.BlockSpec((None, None, t, 2 * FF_PAD), lambda l, w, i: (l, w, i, 0)),
        compiler_params=cp,
        name="cast_gate_up",
    )(w_gate, w_up)
    w_d = pl.pallas_call(
        _cast_down_kernel,
        out_shape=jax.ShapeDtypeStruct((n_l, n_w, FF_PAD, D_MODEL), BF16),
        grid=(n_l, n_w, D_MODEL // t),
        in_specs=[pl.BlockSpec((None, None, D_FF, t), lambda l, w, i: (l, w, 0, i))],
        out_specs=pl.BlockSpec((None, None, FF_PAD, t), lambda l, w, i: (l, w, 0, i)),
        compiler_params=cp,
        name="cast_down",
    )(w_down)
    return w_gu, w_d


def _ada_kernel(c_ref, w_ref, b_ref, o_ref):
    a = _silu(c_ref[...]).astype(BF16)
    o_ref[...] = _dot(a, w_ref[...].astype(BF16)) + b_ref[...]


def _ada_call(c_all, w_ada, b_ada):
    rows = c_all.shape[0]
    n_out = N_ADA * D_MODEL
    return pl.pallas_call(
        _ada_kernel,
        out_shape=jax.ShapeDtypeStruct((DEPTH, rows, n_out), F32),
        grid=(DEPTH, n_out // ADA_TILE),
        in_specs=[
            pl.BlockSpec((rows, D_MODEL), lambda l, j: (0, 0)),
            pl.BlockSpec((None, D_MODEL, ADA_TILE), lambda l, j: (l, 0, j)),
            pl.BlockSpec((None, 1, ADA_TILE), lambda l, j: (l, 0, j)),
        ],
        out_specs=pl.BlockSpec((None, rows, ADA_TILE), lambda l, j: (l, 0, j)),
        compiler_params=_params(("arbitrary", "arbitrary")),
        name="ada_proj",
    )(c_all, w_ada, b_ada.reshape(DEPTH, 1, n_out))


def _x_spec(st, buffers=None):
    mode = None if buffers is None else pl.Buffered(buffers)
    return pl.BlockSpec((st.tile, D_MODEL), lambda i, j: (i, 0), pipeline_mode=mode)


def _norm_spec(layer, sub):
    return pl.BlockSpec((None, 1, D_MODEL), lambda i, j: (layer * 3 + sub, 0, 0))


def _mod_spec(st, k, width=D_MODEL, by_col=False):
    tps = st.tiles_per_seq
    if by_col:
        return pl.BlockSpec((None, None, st.mod_rows, width), lambda i, j: (k, i // tps, 0, j))
    return pl.BlockSpec((None, None, st.mod_rows, width), lambda i, j: (k, i // tps, 0, 0))


def _ffn_kernel(x_ref, g_ref, sh_ref, sc_ref, gt_ref, wgu_ref, wd_ref, *rest, final):
    if final:
        fg_ref, o_ref, h_ref = rest
    else:
        o_ref, h_ref = rest
    j = pl.program_id(1)

    @pl.when(j == 0)
    def _():
        _norm_mod_rows(x_ref, g_ref, sh_ref, sc_ref, h_ref, o_ref)

    gu = _dot(h_ref[...], wgu_ref[...])
    a = (_silu(gu[:, :FF_TILE]) * gu[:, FF_TILE:]).astype(BF16)
    o_ref[...] += _dot(a, wd_ref[...]) * (0.5 * gt_ref[...])

    if final:
        @pl.when(j == pl.num_programs(1) - 1)
        def _():
            o_ref[...] = _rmsnorm(o_ref[...], fg_ref[...])


def _ffn_call(st, x, mods, norm_g, w_gu, w_d, layer, which, final_g=None):
    sub = 2 * which
    in_specs = [
        _x_spec(st),
        _norm_spec(layer, sub),
        _mod_spec(st, 3 * sub), _mod_spec(st, 3 * sub + 1), _mod_spec(st, 3 * sub + 2),
        pl.BlockSpec((None, None, D_MODEL, 2 * FF_TILE), lambda i, j: (layer, which, 0, j)),
        pl.BlockSpec((None, None, FF_TILE, D_MODEL), lambda i, j: (layer, which, j, 0)),
    ]
    args = [x, norm_g, mods, mods, mods, w_gu, w_d]
    if final_g is not None:
        in_specs.append(pl.BlockSpec((1, D_MODEL), lambda i, j: (0, 0)))
        args.append(final_g.reshape(1, D_MODEL))
    return pl.pallas_call(
        functools.partial(_ffn_kernel, final=final_g is not None),
        out_shape=jax.ShapeDtypeStruct((st.rows, D_MODEL), F32),
        grid=(st.rows // st.tile, FF_PAD // FF_TILE),
        in_specs=in_specs,
        out_specs=_x_spec(st),
        scratch_shapes=[pltpu.VMEM((st.tile, D_MODEL), BF16)],
        compiler_params=_params(("arbitrary", "arbitrary")),
        name="ffn",
    )(*args)


def _conv_kernel(x_ref, g_ref, sh_ref, sc_ref, gt_ref, wb_ref, wc_ref, wv_ref, cw_ref, wo_ref,
                 *rest, seg_rows, tiles_per_seq):
    if seg_rows is None:
        o_ref, st_ref, h_ref, ubuf_ref, carry_ref = rest
    else:
        p1_ref, p2_ref, o_ref, st_ref, h_ref, ubuf_ref = rest
    i = pl.program_id(0)
    j = pl.program_id(1)
    tm = x_ref.shape[0]

    @pl.when(j == 0)
    def _():
        _norm_mod_rows(x_ref, g_ref, sh_ref, sc_ref, h_ref, o_ref)

    h = h_ref[...]
    y = None
    for c in range(wb_ref.shape[1] // MIX_CHUNK):
        cols = slice(c * MIX_CHUNK, (c + 1) * MIX_CHUNK)
        u = _dot(h, wc_ref[:, cols]) * _dot(h, wv_ref[:, cols])
        ubuf_ref[SUBLANES:, cols] = u
        if seg_rows is None:
            prev = carry_ref[j, :, cols]
            ubuf_ref[:SUBLANES, cols] = jnp.where(i % tiles_per_seq == 0, jnp.zeros_like(prev), prev)
            tail = u[tm - SUBLANES:, :]
            carry_ref[j, :, cols] = tail
            st_ref[:, cols] = tail
            prev1 = ubuf_ref[SUBLANES - 1:SUBLANES - 1 + tm, cols]
            prev2 = ubuf_ref[SUBLANES - 2:SUBLANES - 2 + tm, cols]
        else:
            ubuf_ref[:SUBLANES, cols] = jnp.zeros((SUBLANES, MIX_CHUNK), F32)
            st_ref[:, cols] = u
            t = lax.broadcasted_iota(jnp.int32, u.shape, 0) % seg_rows
            prev1 = jnp.where(t >= 1, ubuf_ref[SUBLANES - 1:SUBLANES - 1 + tm, cols], p1_ref[:, cols])
            prev2 = jnp.where(t >= 2, ubuf_ref[SUBLANES - 2:SUBLANES - 2 + tm, cols], p2_ref[:, cols])
        conv = cw_ref[0:1, cols] * prev2 + cw_ref[1:2, cols] * prev1 + cw_ref[2:3, cols] * u
        gated = (_dot(h, wb_ref[:, cols]) * conv).astype(BF16)
        part = _dot(gated, wo_ref[cols, :])
        y = part if y is None else y + part
    o_ref[...] += y * gt_ref[...]


def _conv_call(st, x, mods, norm_g, w_in, conv_w, w_out, layer, j_conv, prefix=None, seg_rows=None):
    if st.tile > MIX_ROWS:
        st = _Stream(st.rows, MIX_ROWS, st.tiles_per_seq * (st.tile // MIX_ROWS), st.mod_rows)
    tn = MIX_TILE
    nj = D_MODEL // tn
    n_tiles = st.rows // st.tile
    col = pl.BlockSpec((st.tile, tn), lambda i, j: (i, j))
    in_specs = [
        _x_spec(st),
        _norm_spec(layer, 1),
        _mod_spec(st, 3), _mod_spec(st, 4), _mod_spec(st, 5),
        pl.BlockSpec((None, D_MODEL, tn), lambda i, j: (j_conv, 0, j)),
        pl.BlockSpec((None, D_MODEL, tn), lambda i, j: (j_conv, 0, nj + j)),
        pl.BlockSpec((None, D_MODEL, tn), lambda i, j: (j_conv, 0, 2 * nj + j)),
        pl.BlockSpec((None, CONV_WIDTH, tn), lambda i, j: (j_conv, 0, j)),
        pl.BlockSpec((None, tn, D_MODEL), lambda i, j: (j_conv, j, 0)),
    ]
    args = [x, norm_g, mods, mods, mods, w_in, w_in, w_in, conv_w, w_out]
    scratch = [pltpu.VMEM((st.tile, D_MODEL), BF16), pltpu.VMEM((st.tile + SUBLANES, tn), F32)]
    if prefix is None:
        st_shape = jax.ShapeDtypeStruct((n_tiles, SUBLANES, D_MODEL), F32)
        st_spec = pl.BlockSpec((None, SUBLANES, tn), lambda i, j: (i, 0, j))
        scratch.append(pltpu.VMEM((nj, SUBLANES, tn), F32))
    else:
        in_specs += [col, col]
        args += list(prefix)
        st_shape = jax.ShapeDtypeStruct((st.rows, D_MODEL), F32)
        st_spec = col
    return pl.pallas_call(
        functools.partial(_conv_kernel, seg_rows=seg_rows, tiles_per_seq=st.tiles_per_seq),
        out_shape=(jax.ShapeDtypeStruct((st.rows, D_MODEL), F32), st_shape),
        grid=(n_tiles, nj),
        in_specs=in_specs,
        out_specs=(_x_spec(st), st_spec),
        scratch_shapes=scratch,
        compiler_params=_params(("arbitrary", "arbitrary")),
        name="conv_mixer",
    )(*args)


def _rope_tile(x, cos, sin, first_half):
    outs = []
    for c in range(x.shape[1] // LANES):
        xc = x[:, c * LANES:(c + 1) * LANES]
        partner = jnp.where(first_half, pltpu.roll(xc, LANES - HEAD_DIM // 2, 1),
                            pltpu.roll(xc, HEAD_DIM // 2, 1))
        outs.append(xc * cos + partner * sin)
    return jnp.concatenate(outs, axis=1)


def _qkv_kernel(x_ref, g_ref, sh_ref, sc_ref, wq_ref, wk_ref, wv_ref, cos_ref, sin_ref,
                q_ref, k_ref, vt_ref, kt_ref, vf_ref, h_ref):
    @pl.when(pl.program_id(1) == 0)
    def _():
        _norm_mod_rows(x_ref, g_ref, sh_ref, sc_ref, h_ref)

    h = h_ref[...]
    cos = cos_ref[...]
    sin = sin_ref[...]
    first_half = lax.broadcasted_iota(jnp.int32, cos.shape, 1) % HEAD_DIM < HEAD_DIM // 2
    q = _rope_tile(_dot(h, wq_ref[...]), cos, sin, first_half)
    k = _rope_tile(_dot(h, wk_ref[...]), cos, sin, first_half)
    v = _dot(h, wv_ref[...])
    q_ref[...] = q
    k_ref[...] = k.astype(BF16)
    kt_ref[...] = k.T
    vt_ref[...] = v.T.astype(BF16)
    vf_ref[...] = v


def _qkv_call(st, x, mods, norm_g, w_qkv, cos_t, sin_t, layer, j_attn):
    tn = QKV_TILE
    nj = D_MODEL // tn
    tps = st.tiles_per_seq
    seq_rows = st.tile * tps
    n_seq = st.rows // seq_rows
    col = pl.BlockSpec((st.tile, tn), lambda i, j: (i, j))
    col_t = pl.BlockSpec((None, tn, st.tile), lambda i, j: (i // tps, j, i % tps))
    rope = pl.BlockSpec((st.tile, LANES), lambda i, j: (i % tps, 0))
    return pl.pallas_call(
        _qkv_kernel,
        out_shape=(
            jax.ShapeDtypeStruct((st.rows, D_MODEL), F32),
            jax.ShapeDtypeStruct((st.rows, D_MODEL), BF16),
            jax.ShapeDtypeStruct((n_seq, D_MODEL, seq_rows), BF16),
            jax.ShapeDtypeStruct((n_seq, D_MODEL, seq_rows), F32),
            jax.ShapeDtypeStruct((st.rows, D_MODEL), F32),
        ),
        grid=(st.rows // st.tile, nj),
        in_specs=[
            _x_spec(st),
            _norm_spec(layer, 1),
            _mod_spec(st, 3), _mod_spec(st, 4),
            pl.BlockSpec((None, D_MODEL, tn), lambda i, j: (j_attn, 0, j)),
            pl.BlockSpec((None, D_MODEL, tn), lambda i, j: (j_attn, 0, nj + j)),
            pl.BlockSpec((None, D_MODEL, tn), lambda i, j: (j_attn, 0, 2 * nj + j)),
            rope, rope,
        ],
        out_specs=(col, col, col_t, col_t, col),
        scratch_shapes=[pltpu.VMEM((st.tile, D_MODEL), BF16)],
        compiler_params=_params(("arbitrary", "arbitrary")),
        name="qkv_rope",
    )(x, norm_g, mods, mods, w_qkv, w_qkv, w_qkv, cos_t, sin_t)


def _proj_kernel(a_ref, w_ref, x_ref, gt_ref, o_ref):
    o_ref[...] = x_ref[...] + gt_ref[...] * _dot(a_ref[...], w_ref[...])


def _proj_call(st, a, w_out, x, mods, j_attn):
    tn = PROJ_TILE
    col = pl.BlockSpec((st.tile, tn), lambda i, j: (i, j))
    return pl.pallas_call(
        _proj_kernel,
        out_shape=jax.ShapeDtypeStruct((st.rows, D_MODEL), F32),
        grid=(st.rows // st.tile, D_MODEL // tn),
        in_specs=[
            pl.BlockSpec((st.tile, D_MODEL), lambda i, j: (i, 0)),
            pl.BlockSpec((None, D_MODEL, tn), lambda i, j: (j_attn, 0, j)),
            col,
            _mod_spec(st, 5, width=tn, by_col=True),
        ],
        out_specs=col,
        compiler_params=_params(("arbitrary", "arbitrary")),
        name="out_proj",
    )(a, w_out, x, mods)


def _subln(o, g, lam_init):
    return o * lax.rsqrt(jnp.mean(o * o, axis=-1, keepdims=True) + EPS) * g * (1.0 - lam_init)


def _flash_part(part, n_parts, lam_ref, q_ref, k_ref, vt_ref, sg_ref, o_ref, qt_ref, s_ref, m_ref, l_ref,
                acc_ref, lam_init):
    t = ATTN_TILE
    per_block = n_parts // (q_ref.shape[0] // t)
    qi = part // per_block
    sub = part % per_block
    scale = HEAD_DIM ** -0.5 * math.log2(math.e)

    def scores(ki):
        k0 = pl.multiple_of(ki * t, t)
        return _dot(k_ref[pl.ds(k0, t), :], qt_ref[...])

    def update(st, ki, diagonal):
        if diagonal:
            key = lax.broadcasted_iota(jnp.int32, st.shape, 0)
            qry = lax.broadcasted_iota(jnp.int32, st.shape, 1) % t
            st = jnp.where(key <= qry, st, NEG_INF)
        k0 = pl.multiple_of(ki * t, t)
        m_prev = m_ref[...]
        m_new = jnp.maximum(m_prev, jnp.max(st, axis=0, keepdims=True))
        alpha = jnp.exp2(m_prev - m_new)
        p = jnp.exp2(st - m_new)
        l_ref[...] = alpha * l_ref[...] + jnp.sum(p, axis=0, keepdims=True)
        acc_ref[...] = alpha * acc_ref[...] + _dot(vt_ref[:, pl.ds(k0, t)], p.astype(BF16))
        m_ref[...] = m_new

    def off_diagonal(ki, carry):
        nxt = scores(ki + 1)
        update(s_ref[ki % 2], ki, False)
        s_ref[(ki + 1) % 2] = nxt
        return carry

    q0 = pl.multiple_of(qi * t, t)

    @pl.when(sub == 0)
    def _():
        q = q_ref[pl.ds(q0, t), :] * scale
        lane = lax.broadcasted_iota(jnp.int32, q.shape, 1)
        qt_ref[:, :t] = jnp.where(lane < HEAD_DIM, q, 0.0).T.astype(BF16)
        qt_ref[:, t:] = jnp.where(lane >= HEAD_DIM, q, 0.0).T.astype(BF16)
        m_ref[...] = jnp.full(m_ref.shape, NEG_INF, F32)
        l_ref[...] = jnp.zeros(l_ref.shape, F32)
        acc_ref[...] = jnp.zeros(acc_ref.shape, F32)
        s_ref[0] = scores(0)

    lax.fori_loop(qi * sub // per_block, qi * (sub + 1) // per_block, off_diagonal, 0)

    @pl.when(sub == per_block - 1)
    def _():
        update(s_ref[qi % 2], qi, True)
        ot = acc_ref[...] * (1.0 / l_ref[...])
        d = ot[:, :t] - lam_ref[0] * ot[:, t:]
        d = d * lax.rsqrt(jnp.mean(d * d, axis=0, keepdims=True) + EPS)
        o_ref[pl.ds(q0, t), :] = (d.T * sg_ref[...] * (1.0 - lam_init)).astype(BF16)


def _attn_kernel(pt_ref, lam_ref, qs_ref, kn_ref, vn_ref, e_ref, hm_ref, sg_ref, fq_ref, fk_ref, fvt_ref,
                 *rest, n_pages, t_new, lam_init):
    pp = PAGES_PER_STEP
    k_refs = rest[:pp]
    v_refs = rest[pp:2 * pp]
    o_ref, fo_ref, qbd_ref, s_ref, a_ref, acc_ref, qt_ref, fs_ref, fm_ref, fl_ref, facc_ref = rest[2 * pp:]
    step = pl.program_id(1)
    _flash_part(step, 2 * (n_pages // pp), lam_ref, fq_ref, fk_ref, fvt_ref, sg_ref, fo_ref,
                qt_ref, fs_ref, fm_ref, fl_ref, facc_ref, lam_init)
    k_steps = n_pages // pp
    past = n_pages * PAGE_SIZE
    scale = HEAD_DIM ** -0.5
    n_rows = N_HEADS * t_new

    @pl.when(step == 0)
    def _():
        row = lax.broadcasted_iota(jnp.int32, qbd_ref.shape, 0)
        lane = lax.broadcasted_iota(jnp.int32, qbd_ref.shape, 1)
        q = jnp.zeros(qbd_ref.shape, F32)
        for tq in range(t_new):
            q = jnp.where(row % t_new == tq, qs_ref[tq:tq + 1, :], q)
        q_head = 2 * (row % n_rows // t_new) + row // n_rows
        qbd_ref[...] = jnp.where(lane // HEAD_DIM == q_head, q, 0.0)

    @pl.when(step < k_steps)
    def _():
        qbd = qbd_ref[...]
        for i in range(pp):
            col = pl.multiple_of((step * pp + i) * PAGE_SIZE, PAGE_SIZE)
            s_ref[:, pl.ds(col, PAGE_SIZE)] = _dot(qbd, k_refs[i][...]) * scale

    @pl.when(step == k_steps - 1)
    def _():
        sn = lax.dot_general(qbd_ref[...].astype(BF16), kn_ref[...], (((1,), (1,)), ((), ())),
                             preferred_element_type=F32) * scale
        tq = lax.broadcasted_iota(jnp.int32, sn.shape, 0) % t_new
        tk = lax.broadcasted_iota(jnp.int32, sn.shape, 1)
        s_ref[:, past:] = jnp.where(tk <= tq, sn, NEG_INF)
        sc = s_ref[...]
        p = jnp.exp(sc - jnp.max(sc, axis=1, keepdims=True))
        p = p * (1.0 / jnp.sum(p, axis=1, keepdims=True))
        a_ref[...] = (p[:n_rows] - lam_ref[0] * p[n_rows:]).astype(BF16)
        acc_ref[...] = jnp.zeros(acc_ref.shape, F32)

    @pl.when(step >= k_steps)
    def _():
        cols = [pl.multiple_of(((step - k_steps) * pp + i) * PAGE_SIZE, PAGE_SIZE) for i in range(pp)]
        a = jnp.concatenate([a_ref[:, pl.ds(c, PAGE_SIZE)] for c in cols], axis=0)
        spread = _dot(a, e_ref[...])
        acc = acc_ref[...]
        for i in range(pp):
            acc += _dot(spread[i * n_rows:(i + 1) * n_rows] * hm_ref[...], v_refs[i][...])
        acc_ref[...] = acc

    @pl.when(step == 2 * k_steps - 1)
    def _():
        spread = _dot(a_ref[:, past:], e_ref[...]) * hm_ref[...]
        o = acc_ref[...] + _dot(spread[:, :vn_ref.shape[0]], vn_ref[...])
        o_ref[...] = _subln(o, sg_ref[...], lam_init).astype(BF16)


def _attn_call(page_table, lam, q_new, k_new, v_new, spread, head_mask, subln_g, k_pages, v_pages,
               q_p, k_p, vt_p, lam_init):
    n_seq, n_pages = page_table.shape
    pp = PAGES_PER_STEP
    k_steps = n_pages // pp
    n_rows = head_mask.shape[0]
    t_new = n_rows // N_HEADS
    keys = n_pages * PAGE_SIZE + LANES
    page_rows = k_pages.shape[1]
    n_prompt, _, seq_len = vt_p.shape
    t = ATTN_TILE
    assert n_seq == n_prompt * N_HEADS and (2 * k_steps) % (seq_len // t) == 0

    def head_rows():
        return pl.BlockSpec((seq_len, LANES), lambda b, s, pt: (b // N_HEADS, b % N_HEADS),
                            pipeline_mode=pl.Buffered(1))

    def k_spec(i):
        return pl.BlockSpec(
            (None, page_rows, PAGE_SIZE),
            lambda b, s, pt: (pt[b * n_pages + jnp.minimum(s, k_steps - 1) * pp + i], 0, 0))

    def v_spec(i):
        return pl.BlockSpec(
            (None, page_rows, V_DIM),
            lambda b, s, pt: (pt[b * n_pages + jnp.maximum(s - k_steps, 0) * pp + i], 0, 0))

    def whole(shape):
        return pl.BlockSpec(shape, lambda b, s, pt: (0,) * len(shape))

    def per_seq(shape):
        return pl.BlockSpec((None,) + shape, lambda b, s, pt: (b,) + (0,) * len(shape))

    grid_spec = pltpu.PrefetchScalarGridSpec(
        num_scalar_prefetch=1,
        grid=(n_seq, 2 * k_steps),
        in_specs=[
            pl.BlockSpec(memory_space=pltpu.SMEM),
            per_seq((t_new, D_MODEL)),
            per_seq((LANES, D_MODEL)),
            per_seq((LANES, V_DIM)),
            whole(spread.shape), whole(head_mask.shape), whole((1, V_DIM)),
            head_rows(), head_rows(),
            pl.BlockSpec((None, V_DIM, seq_len), lambda b, s, pt: (b // N_HEADS, b % N_HEADS, 0),
                         pipeline_mode=pl.Buffered(1)),
        ] + [k_spec(i) for i in range(pp)] + [v_spec(i) for i in range(pp)],
        out_specs=(
            per_seq((n_rows, V_DIM)),
            pl.BlockSpec((seq_len, LANES), lambda b, s, pt: (b // N_HEADS, b % N_HEADS)),
        ),
        scratch_shapes=[
            pltpu.VMEM((2 * n_rows, D_MODEL), F32),
            pltpu.VMEM((2 * n_rows, keys), F32), pltpu.VMEM((n_rows, keys), BF16),
            pltpu.VMEM((n_rows, V_DIM), F32),
            pltpu.VMEM((LANES, 2 * t), BF16), pltpu.VMEM((2, t, 2 * t), F32),
            pltpu.VMEM((1, 2 * t), F32), pltpu.VMEM((1, 2 * t), F32), pltpu.VMEM((V_DIM, 2 * t), F32),
        ],
    )
    return pl.pallas_call(
        functools.partial(_attn_kernel, n_pages=n_pages, t_new=t_new, lam_init=lam_init),
        out_shape=(
            jax.ShapeDtypeStruct((n_seq, n_rows, V_DIM), BF16),
            jax.ShapeDtypeStruct(q_p.shape, BF16),
        ),
        grid_spec=grid_spec,
        compiler_params=_params(("arbitrary", "arbitrary")),
        name="diff_attn",
    )(page_table.reshape(-1), lam, q_new, k_new, v_new, spread, head_mask, subln_g, q_p, k_p, vt_p,
      *([k_pages] * pp), *([v_pages] * pp))


def _rope_tables(pos):
    half = HEAD_DIM // 2
    inv = ROPE_THETA ** (-jnp.arange(half, dtype=F32) / half)
    ang = pos.astype(F32)[:, None] * inv[None, :]
    cos = jnp.cos(ang)
    sin = jnp.sin(ang)
    reps = LANES // HEAD_DIM
    return (jnp.tile(jnp.concatenate([cos, cos], axis=1), (1, reps)),
            jnp.tile(jnp.concatenate([-sin, sin], axis=1), (1, reps)))


def _attention(q, k, v, q_p, k_p, vt_p, page_table, lam, subln_g, k_pages, v_pages, lam_init):
    n_seq = page_table.shape[0]
    t_new = q.shape[0] // n_seq
    q_new = q.reshape(n_seq, t_new, D_MODEL)
    k_new = jnp.pad(k.reshape(n_seq, t_new, D_MODEL), ((0, 0), (0, LANES - t_new), (0, 0)))
    v_new = jnp.pad(v.reshape(n_seq, t_new * N_HEADS, V_DIM), ((0, 0), (0, LANES - t_new * N_HEADS), (0, 0)))
    lane = jnp.arange(PAGE_SIZE * N_HEADS)
    spread = (lane[None, :] // N_HEADS == jnp.arange(PAGE_SIZE)[:, None]).astype(BF16)
    row_head = jnp.arange(N_HEADS * t_new) // t_new
    head_mask = (lane[None, :] % N_HEADS == row_head[:, None]).astype(F32)
    o, o_p = _attn_call(page_table, lam, q_new, k_new, v_new, spread, head_mask, subln_g,
                        k_pages, v_pages, q_p, k_p, vt_p, lam_init)
    o = o.reshape(n_seq, N_HEADS, t_new, V_DIM).transpose(0, 2, 1, 3)
    return o.reshape(n_seq * t_new, D_MODEL), o_p


def _to_attention(st, x, mods, pos_tables, p, conv_prefix, seg_rows):
    cos_t, sin_t = pos_tables
    x = _ffn_call(st, x, mods[0], p['norm_g'], p['w_gu'], p['w_d'], 0, 0)
    x, conv_state = _conv_call(st, x, mods[0], p['norm_g'], p['conv_w_in'], p['conv_w'], p['conv_w_out'],
                               0, 0, conv_prefix, seg_rows)
    x = _ffn_call(st, x, mods[0], p['norm_g'], p['w_gu'], p['w_d'], 0, 1)
    x = _ffn_call(st, x, mods[1], p['norm_g'], p['w_gu'], p['w_d'], 1, 0)
    return (x, conv_state) + _qkv_call(st, x, mods[1], p['norm_g'], p['attn_w_qkv'], cos_t, sin_t, 1, 0)


def _from_attention(st, x, o, mods, p):
    x = _proj_call(st, o, p['attn_w_out'], x, mods[1], 0)
    return _ffn_call(st, x, mods[1], p['norm_g'], p['w_gu'], p['w_d'], 1, 1, p['final_g'])


def _k_from_transposed(kt, n_seq, rows_per_seq):
    s, _, rows = kt.shape
    k = kt.reshape(s, 2 * N_HEADS, HEAD_DIM, rows).transpose(0, 3, 1, 2)
    return k.reshape(1, n_seq, rows_per_seq, 2 * N_HEADS, HEAD_DIM)


def kernel(x_prompt, x_sample, c_prompt, c_sample, state_conv, cache_k, cache_v, page_table, norm_g, final_g,
           w_ada, b_ada, ffn_w_gate, ffn_w_up, ffn_w_down, conv_w_in, conv_w, conv_w_out, attn_w_qkv,
           attn_w_out, lambda_q1, lambda_k1, lambda_q2, lambda_k2, subln_g):
    n_prompt, seq_len, _ = x_prompt.shape
    n_sample, t_new, _ = x_sample.shape
    n_pages = page_table.shape[1]
    assert DEPTH == 2 and t_new >= CONV_WIDTH - 1 and seq_len % PROMPT_TILE == 0
    w_gu, w_d = _cast_ffn_weights(ffn_w_gate, ffn_w_up, ffn_w_down)
    p = dict(
        norm_g=norm_g.reshape(DEPTH * 3, 1, D_MODEL),
        final_g=final_g,
        w_gu=w_gu,
        w_d=w_d,
        conv_w_in=conv_w_in.astype(BF16),
        conv_w=conv_w,
        conv_w_out=conv_w_out.astype(BF16),
        attn_w_qkv=attn_w_qkv.astype(BF16),
        attn_w_out=attn_w_out.astype(BF16),
    )
    j_attn = 0
    lam = (jnp.exp(jnp.sum(lambda_q1[j_attn] * lambda_k1[j_attn]))
           - jnp.exp(jnp.sum(lambda_q2[j_attn] * lambda_k2[j_attn])) + _lambda_init(1)).reshape(1).astype(F32)
    sg = subln_g[j_attn].reshape(1, V_DIM)

    n_cond = n_prompt + n_sample
    c_rows = ((n_cond + SUBLANES - 1) // SUBLANES) * SUBLANES
    c_all = jnp.pad(jnp.concatenate([c_prompt, c_sample], axis=0), ((0, c_rows - n_cond), (0, 0)))
    mods = _ada_call(c_all, w_ada, b_ada).reshape(DEPTH, c_rows, N_ADA, D_MODEL)
    mods_p = [mods[l, :n_prompt].transpose(1, 0, 2)[:, :, None, :] for l in range(DEPTH)]
    mods_s = [jnp.repeat(mods[l, n_prompt:n_cond], t_new, axis=0).transpose(1, 0, 2)[:, None]
              for l in range(DEPTH)]

    st_p = _Stream(n_prompt * seq_len, PROMPT_TILE, seq_len // PROMPT_TILE, 1)
    pos_p = _rope_tables(jnp.arange(seq_len, dtype=jnp.int32))
    x_p, cs_p, q_p, k_p, vt_p, kt_p, v_p = _to_attention(
        st_p, x_prompt.reshape(-1, D_MODEL), mods_p, pos_p, p, None, None)

    rows_s = n_sample * t_new
    st_s = _Stream(rows_s, rows_s, 1, rows_s)
    pos_s = _rope_tables(n_pages * PAGE_SIZE + jnp.arange(rows_s, dtype=jnp.int32) % t_new)
    t_idx = (jnp.arange(rows_s) % t_new)[:, None]
    pre0 = jnp.repeat(state_conv[0, :, 0], t_new, axis=0)
    pre1 = jnp.repeat(state_conv[0, :, 1], t_new, axis=0)
    prefix1 = jnp.where(t_idx == 0, pre1, 0.0)
    prefix2 = jnp.where(t_idx == 0, pre0, jnp.where(t_idx == 1, pre1, 0.0))
    k_pages = cache_k[j_attn].transpose(0, 2, 3, 1).reshape(-1, 2 * N_HEADS * HEAD_DIM, PAGE_SIZE)
    v_pages = cache_v[j_attn].reshape(-1, PAGE_SIZE * N_HEADS, V_DIM)
    x_s, u_s, q_s, k_s, _, kt_s, v_s = _to_attention(
        st_s, x_sample.reshape(-1, D_MODEL), mods_s, pos_s, p, (prefix1, prefix2), t_new)

    o_s, o_p = _attention(q_s, k_s, v_s, q_p, k_p, vt_p, page_table, lam, sg,
                          k_pages, v_pages, _lambda_init(1))
    y_p = _from_attention(st_p, x_p, o_p, mods_p, p)
    y_s = _from_attention(st_s, x_s, o_s, mods_s, p)

    keep = CONV_WIDTH - 1
    cs_p = cs_p.reshape(n_prompt, -1, SUBLANES, D_MODEL)[:, -1, SUBLANES - keep:, :]
    return (
        y_p.reshape(n_prompt, seq_len, D_MODEL),
        y_s.reshape(n_sample, t_new, D_MODEL),
        cs_p[None],
        u_s.reshape(n_sample, t_new, D_MODEL)[None, :, t_new - keep:, :],
        _k_from_transposed(kt_p, n_prompt, seq_len),
        v_p.reshape(1, n_prompt, seq_len, N_HEADS, V_DIM),
        _k_from_transposed(kt_s, n_sample, t_new),
        v_s.reshape(1, n_sample, t_new, N_HEADS, V_DIM),
    )
```

```python
import functools
import math

import jax
import jax.numpy as jnp
from jax import lax
from jax.experimental import pallas as pl
from jax.experimental.pallas import tpu as pltpu

F32 = jnp.float32
BF16 = jnp.bfloat16

D_MODEL = 2048
DEPTH = 2
N_HEADS = 16
HEAD_DIM = 64
V_DIM = 128
D_FF = 5504
N_ADA = 9
CONV_WIDTH = 3
ROPE_THETA = 10000.0
EPS = 1e-5
NEG_INF = -1e30
PAGE_SIZE = 128

LANES = 128
SUBLANES = 8
FF_TILE = 512
FF_PAD = ((D_FF + FF_TILE - 1) // FF_TILE) * FF_TILE
PROMPT_TILE = 1024
NORM_ROWS = 32
NORM_UNROLL = 4
MIX_ROWS = 512
MIX_TILE = 512
MIX_CHUNK = 256
QKV_TILE = 512
PROJ_ROWS = 512
PROJ_TILE = 2048
ADA_TILE = 1024
CAST_TILE = 256
ATTN_TILE = 512
PAGES_PER_STEP = 4
VMEM_LIMIT = 60000 * 1024


def _lambda_init(layer_idx):
    return 0.8 - 0.6 * math.exp(-0.3 * layer_idx)


class _Stream:
    def __init__(self, rows, tile, tiles_per_seq, mod_rows):
        self.rows = rows
        self.tile = tile
        self.tiles_per_seq = tiles_per_seq
        self.mod_rows = mod_rows


def _params(sem):
    return pltpu.CompilerParams(dimension_semantics=sem, vmem_limit_bytes=VMEM_LIMIT)


def _silu(x):
    return x * jax.nn.sigmoid(x)


def _rmsnorm(x, g):
    return x * lax.rsqrt(jnp.mean(x * x, axis=-1, keepdims=True) + EPS) * g


def _dot(a, b):
    return jnp.dot(a, b, preferred_element_type=F32)


def _norm_mod_rows(x_ref, g_ref, sh_ref, sc_ref, h_ref, copy_ref=None):
    tm, d = x_ref.shape
    per_row = sh_ref.shape[0] > 1
    col_tiles = [slice(c * LANES, (c + 1) * LANES) for c in range(d // LANES)]

    def body(r, carry):
        rows = pl.ds(pl.multiple_of(r * NORM_ROWS, NORM_ROWS), NORM_ROWS)
        ss = None
        for cols in col_tiles:
            xc = x_ref[rows, cols]
            ss = xc * xc if ss is None else ss + xc * xc
        rs = lax.rsqrt(jnp.sum(ss, axis=-1, keepdims=True) * (1.0 / d) + EPS)
        rs = jnp.broadcast_to(rs, (NORM_ROWS, LANES))
        for cols in col_tiles:
            xc = x_ref[rows, cols]
            shift = sh_ref[rows, cols] if per_row else sh_ref[:, cols]
            scale = sc_ref[rows, cols] if per_row else sc_ref[:, cols]
            gain = g_ref[:, cols] * (1 + scale)
            h_ref[rows, cols] = (xc * rs * gain + shift).astype(BF16)
            if copy_ref is not None:
                copy_ref[rows, cols] = xc
        return carry

    lax.fori_loop(0, tm // NORM_ROWS, body, 0, unroll=NORM_UNROLL)


def _cast_gate_up_kernel(g_ref, u_ref, o_ref):
    rows = o_ref.shape[0]
    for c in range(FF_PAD // FF_TILE):
        lo = c * FF_TILE
        n = min(FF_TILE, D_FF - lo)
        for k, ref in enumerate((g_ref, u_ref)):
            dst = 2 * lo + k * FF_TILE
            o_ref[:, dst:dst + n] = ref[:, lo:lo + n].astype(BF16)
            if n < FF_TILE:
                o_ref[:, dst + n:dst + FF_TILE] = jnp.zeros((rows, FF_TILE - n), BF16)


def _cast_down_kernel(w_ref, o_ref):
    o_ref[:D_FF, :] = w_ref[...].astype(BF16)
    o_ref[D_FF:, :] = jnp.zeros((FF_PAD - D_FF, o_ref.shape[1]), BF16)


def _cast_ffn_weights(w_gate, w_up, w_down):
    n_l, n_w = w_gate.shape[:2]
    t = CAST_TILE
    cp = _params(("arbitrary", "arbitrary", "arbitrary"))
    rows_in = pl.BlockSpec((None, None, t, D_FF), lambda l, w, i: (l, w, i, 0))
    w_gu = pl.pallas_call(
        _cast_gate_up_kernel,
        out_shape=jax.ShapeDtypeStruct((n_l, n_w, D_MODEL, 2 * FF_PAD), BF16),
        grid=(n_l, n_w, D_MODEL // t),
        in_specs=[rows_in, rows_in],
        out_specs=pl.BlockSpec((None, None, t, 2 * FF_PAD), lambda l, w, i: (l, w, i, 0)),
        compiler_params=cp,
        name="cast_gate_up",
    )(w_gate, w_up)
    w_d = pl.pallas_call(
        _cast_down_kernel,
        out_shape=jax.ShapeDtypeStruct((n_l, n_w, FF_PAD, D_MODEL), BF16),
        grid=(n_l, n_w, D_MODEL // t),
        in_specs=[pl.BlockSpec((None, None, D_FF, t), lambda l, w, i: (l, w, 0, i))],
        out_specs=pl.BlockSpec((None, None, FF_PAD, t), lambda l, w, i: (l, w, 0, i)),
        compiler_params=cp,
        name="cast_down",
    )(w_down)
    return w_gu, w_d


def _ada_kernel(c_ref, w_ref, b_ref, o_ref):
    a = _silu(c_ref[...]).astype(BF16)
    o_ref[...] = _dot(a, w_ref[...].astype(BF16)) + b_ref[...]


def _ada_call(c_all, w_ada, b_ada):
    rows = c_all.shape[0]
    n_out = N_ADA * D_MODEL
    return pl.pallas_call(
        _ada_kernel,
        out_shape=jax.ShapeDtypeStruct((DEPTH, rows, n_out), F32),
        grid=(DEPTH, n_out // ADA_TILE),
        in_specs=[
            pl.BlockSpec((rows, D_MODEL), lambda l, j: (0, 0)),
            pl.BlockSpec((None, D_MODEL, ADA_TILE), lambda l, j: (l, 0, j)),
            pl.BlockSpec((None, 1, ADA_TILE), lambda l, j: (l, 0, j)),
        ],
        out_specs=pl.BlockSpec((None, rows, ADA_TILE), lambda l, j: (l, 0, j)),
        compiler_params=_params(("arbitrary", "arbitrary")),
        name="ada_proj",
    )(c_all, w_ada, b_ada.reshape(DEPTH, 1, n_out))


def _x_spec(st, buffers=None):
    mode = None if buffers is None else pl.Buffered(buffers)
    return pl.BlockSpec((st.tile, D_MODEL), lambda i, j: (i, 0), pipeline_mode=mode)


def _norm_spec(layer, sub):
    return pl.BlockSpec((None, 1, D_MODEL), lambda i, j: (layer * 3 + sub, 0, 0))


def _mod_spec(st, k, width=D_MODEL, by_col=False):
    tps = st.tiles_per_seq
    if by_col:
        return pl.BlockSpec((None, None, st.mod_rows, width), lambda i, j: (k, i // tps, 0, j))
    return pl.BlockSpec((None, None, st.mod_rows, width), lambda i, j: (k, i // tps, 0, 0))


def _ffn_kernel(x_ref, g_ref, sh_ref, sc_ref, gt_ref, wgu_ref, wd_ref, *rest, final):
    if final:
        fg_ref, o_ref, h_ref = rest
    else:
        o_ref, h_ref = rest
    j = pl.program_id(1)

    @pl.when(j == 0)
    def _():
        _norm_mod_rows(x_ref, g_ref, sh_ref, sc_ref, h_ref, o_ref)

    gu = _dot(h_ref[...], wgu_ref[...])
    a = (_silu(gu[:, :FF_TILE]) * gu[:, FF_TILE:]).astype(BF16)
    o_ref[...] += _dot(a, wd_ref[...]) * (0.5 * gt_ref[...])

    if final:
        @pl.when(j == pl.num_programs(1) - 1)
        def _():
            o_ref[...] = _rmsnorm(o_ref[...], fg_ref[...])


def _ffn_call(st, x, mods, norm_g, w_gu, w_d, layer, which, final_g=None):
    sub = 2 * which
    in_specs = [
        _x_spec(st),
        _norm_spec(layer, sub),
        _mod_spec(st, 3 * sub), _mod_spec(st, 3 * sub + 1), _mod_spec(st, 3 * sub + 2),
        pl.BlockSpec((None, None, D_MODEL, 2 * FF_TILE), lambda i, j: (layer, which, 0, j)),
        pl.BlockSpec((None, None, FF_TILE, D_MODEL), lambda i, j: (layer, which, j, 0)),
    ]
    args = [x, norm_g, mods, mods, mods, w_gu, w_d]
    if final_g is not None:
        in_specs.append(pl.BlockSpec((1, D_MODEL), lambda i, j: (0, 0)))
        args.append(final_g.reshape(1, D_MODEL))
    return pl.pallas_call(
        functools.partial(_ffn_kernel, final=final_g is not None),
        out_shape=jax.ShapeDtypeStruct((st.rows, D_MODEL), F32),
        grid=(st.rows // st.tile, FF_PAD // FF_TILE),
        in_specs=in_specs,
        out_specs=_x_spec(st),
        scratch_shapes=[pltpu.VMEM((st.tile, D_MODEL), BF16)],
        compiler_params=_params(("arbitrary", "arbitrary")),
        name="ffn",
    )(*args)


def _conv_kernel(x_ref, g_ref, sh_ref, sc_ref, gt_ref, wb_ref, wc_ref, wv_ref, cw_ref, wo_ref,
                 *rest, seg_rows, tiles_per_seq):
    if seg_rows is None:
        o_ref, st_ref, h_ref, ubuf_ref, carry_ref = rest
    else:
        p1_ref, p2_ref, o_ref, st_ref, h_ref, ubuf_ref = rest
    i = pl.program_id(0)
    j = pl.program_id(1)
    tm = x_ref.shape[0]

    @pl.when(j == 0)
    def _():
        _norm_mod_rows(x_ref, g_ref, sh_ref, sc_ref, h_ref, o_ref)

    h = h_ref[...]
    y = None
    for c in range(wb_ref.shape[1] // MIX_CHUNK):
        cols = slice(c * MIX_CHUNK, (c + 1) * MIX_CHUNK)
        u = _dot(h, wc_ref[:, cols]) * _dot(h, wv_ref[:, cols])
        ubuf_ref[SUBLANES:, cols] = u
        if seg_rows is None:
            prev = carry_ref[j, :, cols]
            ubuf_ref[:SUBLANES, cols] = jnp.where(i % tiles_per_seq == 0, jnp.zeros_like(prev), prev)
            tail = u[tm - SUBLANES:, :]
            carry_ref[j, :, cols] = tail
            st_ref[:, cols] = tail
            prev1 = ubuf_ref[SUBLANES - 1:SUBLANES - 1 + tm, cols]
            prev2 = ubuf_ref[SUBLANES - 2:SUBLANES - 2 + tm, cols]
        else:
            ubuf_ref[:SUBLANES, cols] = jnp.zeros((SUBLANES, MIX_CHUNK), F32)
            st_ref[:, cols] = u
            t = lax.broadcasted_iota(jnp.int32, u.shape, 0) % seg_rows
            prev1 = jnp.where(t >= 1, ubuf_ref[SUBLANES - 1:SUBLANES - 1 + tm, cols], p1_ref[:, cols])
            prev2 = jnp.where(t >= 2, ubuf_ref[SUBLANES - 2:SUBLANES - 2 + tm, cols], p2_ref[:, cols])
        conv = cw_ref[0:1, cols] * prev2 + cw_ref[1:2, cols] * prev1 + cw_ref[2:3, cols] * u
        gated = (_dot(h, wb_ref[:, cols]) * conv).astype(BF16)
        part = _dot(gated, wo_ref[cols, :])
        y = part if y is None else y + part
    o_ref[...] += y * gt_ref[...]


def _conv_call(st, x, mods, norm_g, w_in, conv_w, w_out, layer, j_conv, prefix=None, seg_rows=None):
    if st.tile > MIX_ROWS:
        st = _Stream(st.rows, MIX_ROWS, st.tiles_per_seq * (st.tile // MIX_ROWS), st.mod_rows)
    tn = MIX_TILE
    nj = D_MODEL // tn
    n_tiles = st.rows // st.tile
    col = pl.BlockSpec((st.tile, tn), lambda i, j: (i, j))
    in_specs = [
        _x_spec(st),
        _norm_spec(layer, 1),
        _mod_spec(st, 3), _mod_spec(st, 4), _mod_spec(st, 5),
        pl.BlockSpec((None, D_MODEL, tn), lambda i, j: (j_conv, 0, j)),
        pl.BlockSpec((None, D_MODEL, tn), lambda i, j: (j_conv, 0, nj + j)),
        pl.BlockSpec((None, D_MODEL, tn), lambda i, j: (j_conv, 0, 2 * nj + j)),
        pl.BlockSpec((None, CONV_WIDTH, tn), lambda i, j: (j_conv, 0, j)),
        pl.BlockSpec((None, tn, D_MODEL), lambda i, j: (j_conv, j, 0)),
    ]
    args = [x, norm_g, mods, mods, mods, w_in, w_in, w_in, conv_w, w_out]
    scratch = [pltpu.VMEM((st.tile, D_MODEL), BF16), pltpu.VMEM((st.tile + SUBLANES, tn), F32)]
    if prefix is None:
        st_shape = jax.ShapeDtypeStruct((n_tiles, SUBLANES, D_MODEL), F32)
        st_spec = pl.BlockSpec((None, SUBLANES, tn), lambda i, j: (i, 0, j))
        scratch.append(pltpu.VMEM((nj, SUBLANES, tn), F32))
    else:
        in_specs += [col, col]
        args += list(prefix)
        st_shape = jax.ShapeDtypeStruct((st.rows, D_MODEL), F32)
        st_spec = col
    return pl.pallas_call(
        functools.partial(_conv_kernel, seg_rows=seg_rows, tiles_per_seq=st.tiles_per_seq),
        out_shape=(jax.ShapeDtypeStruct((st.rows, D_MODEL), F32), st_shape),
        grid=(n_tiles, nj),
        in_specs=in_specs,
        out_specs=(_x_spec(st), st_spec),
        scratch_shapes=scratch,
        compiler_params=_params(("arbitrary", "arbitrary")),
        name="conv_mixer",
    )(*args)


def _rope_tile(x, cos, sin, first_half):
    outs = []
    for c in range(x.shape[1] // LANES):
        xc = x[:, c * LANES:(c + 1) * LANES]
        partner = jnp.where(first_half, pltpu.roll(xc, LANES - HEAD_DIM // 2, 1),
                            pltpu.roll(xc, HEAD_DIM // 2, 1))
        outs.append(xc * cos + partner * sin)
    return jnp.concatenate(outs, axis=1)


def _qkv_kernel(x_ref, g_ref, sh_ref, sc_ref, wq_ref, wk_ref, wv_ref, cos_ref, sin_ref,
                q_ref, k_ref, vt_ref, kt_ref, vf_ref, h_ref):
    @pl.when(pl.program_id(1) == 0)
    def _():
        _norm_mod_rows(x_ref, g_ref, sh_ref, sc_ref, h_ref)

    h = h_ref[...]
    cos = cos_ref[...]
    sin = sin_ref[...]
    first_half = lax.broadcasted_iota(jnp.int32, cos.shape, 1) % HEAD_DIM < HEAD_DIM // 2
    q = _rope_tile(_dot(h, wq_ref[...]), cos, sin, first_half)
    k = _rope_tile(_dot(h, wk_ref[...]), cos, sin, first_half)
    v = _dot(h, wv_ref[...])
    q_ref[...] = q
    k_ref[...] = k.astype(BF16)
    kt_ref[...] = k.T
    vt_ref[...] = v.T.astype(BF16)
    vf_ref[...] = v


def _qkv_call(st, x, mods, norm_g, w_qkv, cos_t, sin_t, layer, j_attn):
    tn = QKV_TILE
    nj = D_MODEL // tn
    tps = st.tiles_per_seq
    seq_rows = st.tile * tps
    n_seq = st.rows // seq_rows
    col = pl.BlockSpec((st.tile, tn), lambda i, j: (i, j))
    col_t = pl.BlockSpec((None, tn, st.tile), lambda i, j: (i // tps, j, i % tps))
    rope = pl.BlockSpec((st.tile, LANES), lambda i, j: (i % tps, 0))
    return pl.pallas_call(
        _qkv_kernel,
        out_shape=(
            jax.ShapeDtypeStruct((st.rows, D_MODEL), F32),
            jax.ShapeDtypeStruct((st.rows, D_MODEL), BF16),
            jax.ShapeDtypeStruct((n_seq, D_MODEL, seq_rows), BF16),
            jax.ShapeDtypeStruct((n_seq, D_MODEL, seq_rows), F32),
            jax.ShapeDtypeStruct((st.rows, D_MODEL), F32),
        ),
        grid=(st.rows // st.tile, nj),
        in_specs=[
            _x_spec(st),
            _norm_spec(layer, 1),
            _mod_spec(st, 3), _mod_spec(st, 4),
            pl.BlockSpec((None, D_MODEL, tn), lambda i, j: (j_attn, 0, j)),
            pl.BlockSpec((None, D_MODEL, tn), lambda i, j: (j_attn, 0, nj + j)),
            pl.BlockSpec((None, D_MODEL, tn), lambda i, j: (j_attn, 0, 2 * nj + j)),
            rope, rope,
        ],
        out_specs=(col, col, col_t, col_t, col),
        scratch_shapes=[pltpu.VMEM((st.tile, D_MODEL), BF16)],
        compiler_params=_params(("arbitrary", "arbitrary")),
        name="qkv_rope",
    )(x, norm_g, mods, mods, w_qkv, w_qkv, w_qkv, cos_t, sin_t)


def _proj_kernel(a_ref, w_ref, x_ref, gt_ref, o_ref):
    o_ref[...] = x_ref[...] + gt_ref[...] * _dot(a_ref[...], w_ref[...])


def _proj_call(st, a, w_out, x, mods, j_attn):
    if st.tile > PROJ_ROWS:
        st = _Stream(st.rows, PROJ_ROWS, st.tiles_per_seq * (st.tile // PROJ_ROWS), st.mod_rows)
    tn = PROJ_TILE
    col = pl.BlockSpec((st.tile, tn), lambda i, j: (i, j))
    return pl.pallas_call(
        _proj_kernel,
        out_shape=jax.ShapeDtypeStruct((st.rows, D_MODEL), F32),
        grid=(st.rows // st.tile, D_MODEL // tn),
        in_specs=[
            pl.BlockSpec((st.tile, D_MODEL), lambda i, j: (i, 0)),
            pl.BlockSpec((None, D_MODEL, tn), lambda i, j: (j_attn, 0, j)),
            col,
            _mod_spec(st, 5, width=tn, by_col=True),
        ],
        out_specs=col,
        compiler_params=_params(("arbitrary", "arbitrary")),
        name="out_proj",
    )(a, w_out, x, mods)


def _subln(o, g, lam_init):
    return o * lax.rsqrt(jnp.mean(o * o, axis=-1, keepdims=True) + EPS) * g * (1.0 - lam_init)


def _flash_part(qi, sub, per_block, lam_ref, q_ref, k_ref, vt_ref, sg_ref, o_ref, qt_ref, s_ref, m_ref, l_ref,
                acc_ref, lam_init):
    t = ATTN_TILE
    scale = HEAD_DIM ** -0.5 * math.log2(math.e)

    def scores(ki):
        k0 = pl.multiple_of(ki * t, t)
        return _dot(k_ref[pl.ds(k0, t), :], qt_ref[...])

    def update(st, ki, diagonal):
        if diagonal:
            key = lax.broadcasted_iota(jnp.int32, st.shape, 0)
            qry = lax.broadcasted_iota(jnp.int32, st.shape, 1) % t
            st = jnp.where(key <= qry, st, NEG_INF)
        k0 = pl.multiple_of(ki * t, t)
        m_prev = m_ref[...]
        m_new = jnp.maximum(m_prev, jnp.max(st, axis=0, keepdims=True))
        alpha = jnp.exp2(m_prev - m_new)
        p = jnp.exp2(st - m_new)
        l_ref[...] = alpha * l_ref[...] + jnp.sum(p, axis=0, keepdims=True)
        acc_ref[...] = alpha * acc_ref[...] + _dot(vt_ref[:, pl.ds(k0, t)], p.astype(BF16))
        m_ref[...] = m_new

    def off_diagonal(ki, carry):
        nxt = scores(ki + 1)
        update(s_ref[ki % 2], ki, False)
        s_ref[(ki + 1) % 2] = nxt
        return carry

    q0 = pl.multiple_of(qi * t, t)

    @pl.when(sub == 0)
    def _():
        q = q_ref[pl.ds(q0, t), :] * scale
        lane = lax.broadcasted_iota(jnp.int32, q.shape, 1)
        qt_ref[:, :t] = jnp.where(lane < HEAD_DIM, q, 0.0).T.astype(BF16)
        qt_ref[:, t:] = jnp.where(lane >= HEAD_DIM, q, 0.0).T.astype(BF16)
        m_ref[...] = jnp.full(m_ref.shape, NEG_INF, F32)
        l_ref[...] = jnp.zeros(l_ref.shape, F32)
        acc_ref[...] = jnp.zeros(acc_ref.shape, F32)
        s_ref[0] = scores(0)

    lax.fori_loop(qi * sub // per_block, qi * (sub + 1) // per_block, off_diagonal, 0)

    @pl.when(sub == per_block - 1)
    def _():
        update(s_ref[qi % 2], qi, True)
        ot = acc_ref[...] * (1.0 / l_ref[...])
        d = ot[:, :t] - lam_ref[0] * ot[:, t:]
        d = d * lax.rsqrt(jnp.mean(d * d, axis=0, keepdims=True) + EPS)
        o_ref[pl.ds(q0, t), :] = (d.T * sg_ref[...] * (1.0 - lam_init)).astype(BF16)


def _attn_kernel(pt_ref, plan_ref, lam_ref, qs_ref, kn_ref, vn_ref, sg_ref, fq_ref, fk_ref, fvt_ref,
                 *rest, n_pages, t_new, lam_init):
    pp = PAGES_PER_STEP
    k_refs = rest[:pp]
    v_refs = rest[pp:2 * pp]
    (o_ref, fo_ref, e_ref, hm_ref, qbd_ref, s_ref, a_ref, acc_ref,
     qt_ref, fs_ref, fm_ref, fl_ref, facc_ref) = rest[2 * pp:]
    step = pl.program_id(1)
    n_steps = 2 * (n_pages // pp)
    _flash_part(plan_ref[step], plan_ref[n_steps + step], plan_ref[2 * n_steps + step],
                lam_ref, fq_ref, fk_ref, fvt_ref, sg_ref, fo_ref,
                qt_ref, fs_ref, fm_ref, fl_ref, facc_ref, lam_init)
    k_steps = n_pages // pp
    past = n_pages * PAGE_SIZE
    scale = HEAD_DIM ** -0.5
    n_rows = N_HEADS * t_new

    @pl.when(jnp.logical_and(pl.program_id(0) == 0, step == 0))
    def _():
        tok = lax.broadcasted_iota(jnp.int32, e_ref.shape, 0)
        lane = lax.broadcasted_iota(jnp.int32, e_ref.shape, 1)
        e_ref[...] = jnp.where(lane // N_HEADS == tok, 1.0, 0.0).astype(BF16)
        row = lax.broadcasted_iota(jnp.int32, hm_ref.shape, 0)
        lane = lax.broadcasted_iota(jnp.int32, hm_ref.shape, 1)
        hm_ref[...] = jnp.where(lane % N_HEADS == row // t_new, 1.0, 0.0)

    @pl.when(step == 0)
    def _():
        row = lax.broadcasted_iota(jnp.int32, qbd_ref.shape, 0)
        lane = lax.broadcasted_iota(jnp.int32, qbd_ref.shape, 1)
        q = jnp.zeros(qbd_ref.shape, F32)
        for tq in range(t_new):
            q = jnp.where(row % t_new == tq, qs_ref[tq:tq + 1, :], q)
        q_head = 2 * (row % n_rows // t_new) + row // n_rows
        qbd_ref[...] = jnp.where(lane // HEAD_DIM == q_head, q, 0.0)

    @pl.when(step < k_steps)
    def _():
        qbd = qbd_ref[...]
        for i in range(pp):
            col = pl.multiple_of((step * pp + i) * PAGE_SIZE, PAGE_SIZE)
            s_ref[:, pl.ds(col, PAGE_SIZE)] = _dot(qbd, k_refs[i][...]) * scale

    @pl.when(step == k_steps - 1)
    def _():
        sn = lax.dot_general(qbd_ref[...].astype(BF16), kn_ref[...], (((1,), (1,)), ((), ())),
                             preferred_element_type=F32) * scale
        tq = lax.broadcasted_iota(jnp.int32, sn.shape, 0) % t_new
        tk = lax.broadcasted_iota(jnp.int32, sn.shape, 1)
        s_ref[:, past:] = jnp.where(tk <= tq, sn, NEG_INF)
        sc = s_ref[...]
        p = jnp.exp(sc - jnp.max(sc, axis=1, keepdims=True))
        p = p * (1.0 / jnp.sum(p, axis=1, keepdims=True))
        a_ref[...] = (p[:n_rows] - lam_ref[0] * p[n_rows:]).astype(BF16)
        acc_ref[...] = jnp.zeros(acc_ref.shape, F32)

    @pl.when(step >= k_steps)
    def _():
        cols = [pl.multiple_of(((step - k_steps) * pp + i) * PAGE_SIZE, PAGE_SIZE) for i in range(pp)]
        a = jnp.concatenate([a_ref[:, pl.ds(c, PAGE_SIZE)] for c in cols], axis=0)
        spread = _dot(a, e_ref[...])
        acc = acc_ref[...]
        for i in range(pp):
            acc += _dot(spread[i * n_rows:(i + 1) * n_rows] * hm_ref[...], v_refs[i][...])
        acc_ref[...] = acc

    @pl.when(step == 2 * k_steps - 1)
    def _():
        spread = _dot(a_ref[:, past:], e_ref[...]) * hm_ref[...]
        o = acc_ref[...] + _dot(spread[:, :vn_ref.shape[0]], vn_ref[...])
        o_ref[...] = _subln(o, sg_ref[...], lam_init).astype(BF16)


def _attn_call(page_table, lam, q_new, k_new, v_new, subln_g, k_pages, v_pages,
               q_p, k_p, vt_p, lam_init):
    n_seq, n_pages = page_table.shape
    pp = PAGES_PER_STEP
    k_steps = n_pages // pp
    t_new = q_new.shape[1]
    n_rows = N_HEADS * t_new
    keys = n_pages * PAGE_SIZE + LANES
    page_rows = k_pages.shape[1]
    n_prompt, _, seq_len = vt_p.shape
    t = ATTN_TILE
    n_steps = 2 * k_steps
    n_blocks = seq_len // t
    assert n_seq == n_prompt * N_HEADS and n_steps >= n_blocks
    weight = [qi + 2 for qi in range(n_blocks)]
    parts = [max(1, round(w * n_steps / sum(weight))) for w in weight]
    while sum(parts) > n_steps:
        parts[parts.index(max(parts))] -= 1
    parts[-1] += n_steps - sum(parts)
    plan = ([qi for qi, n in enumerate(parts) for _ in range(n)]
            + [sub for n in parts for sub in range(n)]
            + [n for n in parts for _ in range(n)])

    def head_rows():
        return pl.BlockSpec((seq_len, LANES), lambda b, s, pt, plan: (b // N_HEADS, b % N_HEADS),
                            pipeline_mode=pl.Buffered(1))

    def k_spec(i):
        return pl.BlockSpec(
            (None, page_rows, PAGE_SIZE),
            lambda b, s, pt, plan: (pt[b * n_pages + jnp.minimum(s, k_steps - 1) * pp + i], 0, 0))

    def v_spec(i):
        return pl.BlockSpec(
            (None, page_rows, V_DIM),
            lambda b, s, pt, plan: (pt[b * n_pages + jnp.maximum(s - k_steps, 0) * pp + i], 0, 0))

    def whole(shape):
        return pl.BlockSpec(shape, lambda b, s, pt, plan: (0,) * len(shape))

    def per_seq(shape):
        return pl.BlockSpec((None,) + shape, lambda b, s, pt, plan: (b,) + (0,) * len(shape))

    grid_spec = pltpu.PrefetchScalarGridSpec(
        num_scalar_prefetch=2,
        grid=(n_seq, n_steps),
        in_specs=[
            pl.BlockSpec(memory_space=pltpu.SMEM),
            per_seq((t_new, D_MODEL)),
            per_seq((LANES, D_MODEL)),
            per_seq((LANES, V_DIM)),
            whole((1, V_DIM)),
            head_rows(), head_rows(),
            pl.BlockSpec((None, V_DIM, seq_len), lambda b, s, pt, plan: (b // N_HEADS, b % N_HEADS, 0),
                         pipeline_mode=pl.Buffered(1)),
        ] + [k_spec(i) for i in range(pp)] + [v_spec(i) for i in range(pp)],
        out_specs=(
            per_seq((n_rows, V_DIM)),
            pl.BlockSpec((seq_len, LANES), lambda b, s, pt, plan: (b // N_HEADS, b % N_HEADS)),
        ),
        scratch_shapes=[
            pltpu.VMEM((PAGE_SIZE, PAGE_SIZE * N_HEADS), BF16), pltpu.VMEM((n_rows, PAGE_SIZE * N_HEADS), F32),
            pltpu.VMEM((2 * n_rows, D_MODEL), F32),
            pltpu.VMEM((2 * n_rows, keys), F32), pltpu.VMEM((n_rows, keys), BF16),
            pltpu.VMEM((n_rows, V_DIM), F32),
            pltpu.VMEM((LANES, 2 * t), BF16), pltpu.VMEM((2, t, 2 * t), F32),
            pltpu.VMEM((1, 2 * t), F32), pltpu.VMEM((1, 2 * t), F32), pltpu.VMEM((V_DIM, 2 * t), F32),
        ],
    )
    return pl.pallas_call(
        functools.partial(_attn_kernel, n_pages=n_pages, t_new=t_new, lam_init=lam_init),
        out_shape=(
            jax.ShapeDtypeStruct((n_seq, n_rows, V_DIM), BF16),
            jax.ShapeDtypeStruct(q_p.shape, BF16),
        ),
        grid_spec=grid_spec,
        compiler_params=_params(("arbitrary", "arbitrary")),
        name="diff_attn",
    )(page_table.reshape(-1), jnp.asarray(plan, jnp.int32), lam, q_new, k_new, v_new, subln_g, q_p, k_p, vt_p,
      *([k_pages] * pp), *([v_pages] * pp))


def _rope_tables(pos):
    half = HEAD_DIM // 2
    inv = ROPE_THETA ** (-jnp.arange(half, dtype=F32) / half)
    ang = pos.astype(F32)[:, None] * inv[None, :]
    cos = jnp.cos(ang)
    sin = jnp.sin(ang)
    reps = LANES // HEAD_DIM
    return (jnp.tile(jnp.concatenate([cos, cos], axis=1), (1, reps)),
            jnp.tile(jnp.concatenate([-sin, sin], axis=1), (1, reps)))


def _attention(q, k, v, q_p, k_p, vt_p, page_table, lam, subln_g, k_pages, v_pages, lam_init):
    n_seq = page_table.shape[0]
    t_new = q.shape[0] // n_seq
    q_new = q.reshape(n_seq, t_new, D_MODEL)
    k_new = jnp.pad(k.reshape(n_seq, t_new, D_MODEL), ((0, 0), (0, LANES - t_new), (0, 0)))
    v_new = jnp.pad(v.reshape(n_seq, t_new * N_HEADS, V_DIM), ((0, 0), (0, LANES - t_new * N_HEADS), (0, 0)))
    o, o_p = _attn_call(page_table, lam, q_new, k_new, v_new, subln_g,
                        k_pages, v_pages, q_p, k_p, vt_p, lam_init)
    o = o.reshape(n_seq, N_HEADS, t_new, V_DIM).transpose(0, 2, 1, 3)
    return o.reshape(n_seq * t_new, D_MODEL), o_p


def _to_attention(st, x, mods, pos_tables, p, conv_prefix, seg_rows):
    cos_t, sin_t = pos_tables
    x = _ffn_call(st, x, mods[0], p['norm_g'], p['w_gu'], p['w_d'], 0, 0)
    x, conv_state = _conv_call(st, x, mods[0], p['norm_g'], p['conv_w_in'], p['conv_w'], p['conv_w_out'],
                               0, 0, conv_prefix, seg_rows)
    x = _ffn_call(st, x, mods[0], p['norm_g'], p['w_gu'], p['w_d'], 0, 1)
    x = _ffn_call(st, x, mods[1], p['norm_g'], p['w_gu'], p['w_d'], 1, 0)
    return (x, conv_state) + _qkv_call(st, x, mods[1], p['norm_g'], p['attn_w_qkv'], cos_t, sin_t, 1, 0)


def _from_attention(st, x, o, mods, p):
    x = _proj_call(st, o, p['attn_w_out'], x, mods[1], 0)
    return _ffn_call(st, x, mods[1], p['norm_g'], p['w_gu'], p['w_d'], 1, 1, p['final_g'])


def _k_from_transposed(kt, n_seq, rows_per_seq):
    s, _, rows = kt.shape
    k = kt.reshape(s, 2 * N_HEADS, HEAD_DIM, rows).transpose(0, 3, 1, 2)
    return k.reshape(1, n_seq, rows_per_seq, 2 * N_HEADS, HEAD_DIM)


def kernel(x_prompt, x_sample, c_prompt, c_sample, state_conv, cache_k, cache_v, page_table, norm_g, final_g,
           w_ada, b_ada, ffn_w_gate, ffn_w_up, ffn_w_down, conv_w_in, conv_w, conv_w_out, attn_w_qkv,
           attn_w_out, lambda_q1, lambda_k1, lambda_q2, lambda_k2, subln_g):
    n_prompt, seq_len, _ = x_prompt.shape
    n_sample, t_new, _ = x_sample.shape
    n_pages = page_table.shape[1]
    assert DEPTH == 2 and t_new >= CONV_WIDTH - 1 and seq_len % PROMPT_TILE == 0
    w_gu, w_d = _cast_ffn_weights(ffn_w_gate, ffn_w_up, ffn_w_down)
    p = dict(
        norm_g=norm_g.reshape(DEPTH * 3, 1, D_MODEL),
        final_g=final_g,
        w_gu=w_gu,
        w_d=w_d,
        conv_w_in=conv_w_in.astype(BF16),
        conv_w=conv_w,
        conv_w_out=conv_w_out.astype(BF16),
        attn_w_qkv=attn_w_qkv.astype(BF16),
        attn_w_out=attn_w_out.astype(BF16),
    )
    j_attn = 0
    lam = (jnp.exp(jnp.sum(lambda_q1[j_attn] * lambda_k1[j_attn]))
           - jnp.exp(jnp.sum(lambda_q2[j_attn] * lambda_k2[j_attn])) + _lambda_init(1)).reshape(1).astype(F32)
    sg = subln_g[j_attn].reshape(1, V_DIM)

    n_cond = n_prompt + n_sample
    c_rows = ((n_cond + SUBLANES - 1) // SUBLANES) * SUBLANES
    c_all = jnp.pad(jnp.concatenate([c_prompt, c_sample], axis=0), ((0, c_rows - n_cond), (0, 0)))
    mods = _ada_call(c_all, w_ada, b_ada).reshape(DEPTH, c_rows, N_ADA, D_MODEL)
    mods_p = [mods[l, :n_prompt].transpose(1, 0, 2)[:, :, None, :] for l in range(DEPTH)]
    mods_s = [jnp.repeat(mods[l, n_prompt:n_cond], t_new, axis=0).transpose(1, 0, 2)[:, None]
              for l in range(DEPTH)]

    st_p = _Stream(n_prompt * seq_len, PROMPT_TILE, seq_len // PROMPT_TILE, 1)
    pos_p = _rope_tables(jnp.arange(seq_len, dtype=jnp.int32))
    x_p, cs_p, q_p, k_p, vt_p, kt_p, v_p = _to_attention(
        st_p, x_prompt.reshape(-1, D_MODEL), mods_p, pos_p, p, None, None)

    rows_s = n_sample * t_new
    st_s = _Stream(rows_s, rows_s, 1, rows_s)
    pos_s = _rope_tables(n_pages * PAGE_SIZE + jnp.arange(rows_s, dtype=jnp.int32) % t_new)
    t_idx = (jnp.arange(rows_s) % t_new)[:, None]
    pre0 = jnp.repeat(state_conv[0, :, 0], t_new, axis=0)
    pre1 = jnp.repeat(state_conv[0, :, 1], t_new, axis=0)
    prefix1 = jnp.where(t_idx == 0, pre1, 0.0)
    prefix2 = jnp.where(t_idx == 0, pre0, jnp.where(t_idx == 1, pre1, 0.0))
    k_pages = cache_k[j_attn].transpose(0, 2, 3, 1).reshape(-1, 2 * N_HEADS * HEAD_DIM, PAGE_SIZE)
    v_pages = cache_v[j_attn].reshape(-1, PAGE_SIZE * N_HEADS, V_DIM)
    x_s, u_s, q_s, k_s, _, kt_s, v_s = _to_attention(
        st_s, x_sample.reshape(-1, D_MODEL), mods_s, pos_s, p, (prefix1, prefix2), t_new)

    o_s, o_p = _attention(q_s, k_s, v_s, q_p, k_p, vt_p, page_table, lam, sg,
                          k_pages, v_pages, _lambda_init(1))
    y_p = _from_attention(st_p, x_p, o_p, mods_p, p)
    y_s = _from_attention(st_s, x_s, o_s, mods_s, p)

    keep = CONV_WIDTH - 1
    cs_p = cs_p.reshape(n_prompt, -1, SUBLANES, D_MODEL)[:, -1, SUBLANES - keep:, :]
    return (
        y_p.reshape(n_prompt, seq_len, D_MODEL),
        y_s.reshape(n_sample, t_new, D_MODEL),
        cs_p[None],
        u_s.reshape(n_sample, t_new, D_MODEL)[None, :, t_new - keep:, :],
        _k_from_transposed(kt_p, n_prompt, seq_len),
        v_p.reshape(1, n_prompt, seq_len, N_HEADS, V_DIM),
        _k_from_transposed(kt_s, n_sample, t_new),
        v_s.reshape(1, n_sample, t_new, N_HEADS, V_DIM),
    )
```

```python
import functools
import math

import jax
import jax.numpy as jnp
from jax import lax
from jax.experimental import pallas as pl
from jax.experimental.pallas import tpu as pltpu

F32 = jnp.float32
BF16 = jnp.bfloat16

D_MODEL = 2048
DEPTH = 2
N_HEADS = 16
HEAD_DIM = 64
V_DIM = 128
D_FF = 5504
N_ADA = 9
CONV_WIDTH = 3
ROPE_THETA = 10000.0
EPS = 1e-5
NEG_INF = -1e30
PAGE_SIZE = 128

LANES = 128
SUBLANES = 8
FF_TILE = 512
FF_PAD = ((D_FF + FF_TILE - 1) // FF_TILE) * FF_TILE
PROMPT_TILE = 1024
NORM_ROWS = 32
NORM_UNROLL = 4
MIX_ROWS = 512
MIX_TILE = 512
MIX_CHUNK = 256
QKV_TILE = 512
PROJ_ROWS = 512
PROJ_TILE = 2048
ADA_TILE = 1024
CAST_TILE = 256
ATTN_TILE = 512
PAGES_PER_STEP = 4
VMEM_LIMIT = 60000 * 1024


def _lambda_init(layer_idx):
    return 0.8 - 0.6 * math.exp(-0.3 * layer_idx)


class _Stream:
    def __init__(self, rows, tile, tiles_per_seq, mod_rows):
        self.rows = rows
        self.tile = tile
        self.tiles_per_seq = tiles_per_seq
        self.mod_rows = mod_rows


def _params(sem):
    return pltpu.CompilerParams(dimension_semantics=sem, vmem_limit_bytes=VMEM_LIMIT)


def _silu(x):
    return x * jax.nn.sigmoid(x)


def _rmsnorm(x, g):
    return x * lax.rsqrt(jnp.mean(x * x, axis=-1, keepdims=True) + EPS) * g


def _dot(a, b):
    return jnp.dot(a, b, preferred_element_type=F32)


def _norm_mod_rows(x_ref, g_ref, sh_ref, sc_ref, h_ref, copy_ref=None):
    tm, d = x_ref.shape
    per_row = sh_ref.shape[0] > 1
    col_tiles = [slice(c * LANES, (c + 1) * LANES) for c in range(d // LANES)]

    def body(r, carry):
        rows = pl.ds(pl.multiple_of(r * NORM_ROWS, NORM_ROWS), NORM_ROWS)
        ss = None
        for cols in col_tiles:
            xc = x_ref[rows, cols]
            ss = xc * xc if ss is None else ss + xc * xc
        rs = lax.rsqrt(jnp.sum(ss, axis=-1, keepdims=True) * (1.0 / d) + EPS)
        rs = jnp.broadcast_to(rs, (NORM_ROWS, LANES))
        for cols in col_tiles:
            xc = x_ref[rows, cols]
            shift = sh_ref[rows, cols] if per_row else sh_ref[:, cols]
            scale = sc_ref[rows, cols] if per_row else sc_ref[:, cols]
            gain = g_ref[:, cols] * (1 + scale)
            h_ref[rows, cols] = (xc * rs * gain + shift).astype(BF16)
            if copy_ref is not None:
                copy_ref[rows, cols] = xc
        return carry

    lax.fori_loop(0, tm // NORM_ROWS, body, 0, unroll=NORM_UNROLL)


def _cast_gate_up_kernel(g_ref, u_ref, o_ref):
    rows = o_ref.shape[0]
    for c in range(FF_PAD // FF_TILE):
        lo = c * FF_TILE
        n = min(FF_TILE, D_FF - lo)
        for k, ref in enumerate((g_ref, u_ref)):
            dst = 2 * lo + k * FF_TILE
            o_ref[:, dst:dst + n] = ref[:, lo:lo + n].astype(BF16)
            if n < FF_TILE:
                o_ref[:, dst + n:dst + FF_TILE] = jnp.zeros((rows, FF_TILE - n), BF16)


def _cast_down_kernel(w_ref, o_ref):
    o_ref[:D_FF, :] = w_ref[...].astype(BF16)
    o_ref[D_FF:, :] = jnp.zeros((FF_PAD - D_FF, o_ref.shape[1]), BF16)


def _cast_ffn_weights(w_gate, w_up, w_down):
    n_l, n_w = w_gate.shape[:2]
    t = CAST_TILE
    cp = _params(("arbitrary", "arbitrary", "arbitrary"))
    rows_in = pl.BlockSpec((None, None, t, D_FF), lambda l, w, i: (l, w, i, 0))
    w_gu = pl.pallas_call(
        _cast_gate_up_kernel,
        out_shape=jax.ShapeDtypeStruct((n_l, n_w, D_MODEL, 2 * FF_PAD), BF16),
        grid=(n_l, n_w, D_MODEL // t),
        in_specs=[rows_in, rows_in],
        out_specs=pl.BlockSpec((None, None, t, 2 * FF_PAD), lambda l, w, i: (l, w, i, 0)),
        compiler_params=cp,
        name="cast_gate_up",
    )(w_gate, w_up)
    w_d = pl.pallas_call(
        _cast_down_kernel,
        out_shape=jax.ShapeDtypeStruct((n_l, n_w, FF_PAD, D_MODEL), BF16),
        grid=(n_l, n_w, D_MODEL // t),
        in_specs=[pl.BlockSpec((None, None, D_FF, t), lambda l, w, i: (l, w, 0, i))],
        out_specs=pl.BlockSpec((None, None, FF_PAD, t), lambda l, w, i: (l, w, 0, i)),
        compiler_params=cp,
        name="cast_down",
    )(w_down)
    return w_gu, w_d


def _ada_kernel(c_ref, w_ref, b_ref, o_ref):
    a = _silu(c_ref[...]).astype(BF16)
    o_ref[...] = _dot(a, w_ref[...].astype(BF16)) + b_ref[...]


def _ada_call(c_all, w_ada, b_ada):
    rows = c_all.shape[0]
    n_out = N_ADA * D_MODEL
    return pl.pallas_call(
        _ada_kernel,
        out_shape=jax.ShapeDtypeStruct((DEPTH, rows, n_out), F32),
        grid=(DEPTH, n_out // ADA_TILE),
        in_specs=[
            pl.BlockSpec((rows, D_MODEL), lambda l, j: (0, 0)),
            pl.BlockSpec((None, D_MODEL, ADA_TILE), lambda l, j: (l, 0, j)),
            pl.BlockSpec((None, 1, ADA_TILE), lambda l, j: (l, 0, j)),
        ],
        out_specs=pl.BlockSpec((None, rows, ADA_TILE), lambda l, j: (l, 0, j)),
        compiler_params=_params(("arbitrary", "arbitrary")),
        name="ada_proj",
    )(c_all, w_ada, b_ada.reshape(DEPTH, 1, n_out))


def _x_spec(st, buffers=None):
    mode = None if buffers is None else pl.Buffered(buffers)
    return pl.BlockSpec((st.tile, D_MODEL), lambda i, j: (i, 0), pipeline_mode=mode)


def _norm_spec(layer, sub):
    return pl.BlockSpec((None, 1, D_MODEL), lambda i, j: (layer * 3 + sub, 0, 0))


def _mod_spec(st, k, width=D_MODEL, by_col=False):
    tps = st.tiles_per_seq
    if by_col:
        return pl.BlockSpec((None, None, st.mod_rows, width), lambda i, j: (k, i // tps, 0, j))
    return pl.BlockSpec((None, None, st.mod_rows, width), lambda i, j: (k, i // tps, 0, 0))


def _ffn_kernel(x_ref, g_ref, sh_ref, sc_ref, gt_ref, wgu_ref, wd_ref, *rest, final):
    if final:
        fg_ref, o_ref, h_ref = rest
    else:
        o_ref, h_ref = rest
    j = pl.program_id(1)

    @pl.when(j == 0)
    def _():
        _norm_mod_rows(x_ref, g_ref, sh_ref, sc_ref, h_ref, o_ref)

    gu = _dot(h_ref[...], wgu_ref[...])
    a = (_silu(gu[:, :FF_TILE]) * gu[:, FF_TILE:]).astype(BF16)
    o_ref[...] += _dot(a, wd_ref[...]) * (0.5 * gt_ref[...])

    if final:
        @pl.when(j == pl.num_programs(1) - 1)
        def _():
            o_ref[...] = _rmsnorm(o_ref[...], fg_ref[...])


def _ffn_call(st, x, mods, norm_g, w_gu, w_d, layer, which, final_g=None):
    sub = 2 * which
    in_specs = [
        _x_spec(st),
        _norm_spec(layer, sub),
        _mod_spec(st, 3 * sub), _mod_spec(st, 3 * sub + 1), _mod_spec(st, 3 * sub + 2),
        pl.BlockSpec((None, None, D_MODEL, 2 * FF_TILE), lambda i, j: (layer, which, 0, j)),
        pl.BlockSpec((None, None, FF_TILE, D_MODEL), lambda i, j: (layer, which, j, 0)),
    ]
    args = [x, norm_g, mods, mods, mods, w_gu, w_d]
    if final_g is not None:
        in_specs.append(pl.BlockSpec((1, D_MODEL), lambda i, j: (0, 0)))
        args.append(final_g.reshape(1, D_MODEL))
    return pl.pallas_call(
        functools.partial(_ffn_kernel, final=final_g is not None),
        out_shape=jax.ShapeDtypeStruct((st.rows, D_MODEL), F32),
        grid=(st.rows // st.tile, FF_PAD // FF_TILE),
        in_specs=in_specs,
        out_specs=_x_spec(st),
        scratch_shapes=[pltpu.VMEM((st.tile, D_MODEL), BF16)],
        compiler_params=_params(("arbitrary", "arbitrary")),
        name="ffn",
    )(*args)


def _conv_kernel(x_ref, g_ref, sh_ref, sc_ref, gt_ref, wb_ref, wc_ref, wv_ref, cw_ref, wo_ref,
                 *rest, seg_rows, tiles_per_seq):
    if seg_rows is None:
        o_ref, st_ref, h_ref, ubuf_ref, carry_ref = rest
    else:
        p1_ref, p2_ref, o_ref, st_ref, h_ref, ubuf_ref = rest
    i = pl.program_id(0)
    j = pl.program_id(1)
    tm = x_ref.shape[0]

    @pl.when(j == 0)
    def _():
        _norm_mod_rows(x_ref, g_ref, sh_ref, sc_ref, h_ref, o_ref)

    h = h_ref[...]
    y = None
    for c in range(wb_ref.shape[1] // MIX_CHUNK):
        cols = slice(c * MIX_CHUNK, (c + 1) * MIX_CHUNK)
        u = _dot(h, wc_ref[:, cols]) * _dot(h, wv_ref[:, cols])
        ubuf_ref[SUBLANES:, cols] = u
        if seg_rows is None:
            prev = carry_ref[j, :, cols]
            ubuf_ref[:SUBLANES, cols] = jnp.where(i % tiles_per_seq == 0, jnp.zeros_like(prev), prev)
            tail = u[tm - SUBLANES:, :]
            carry_ref[j, :, cols] = tail
            st_ref[:, cols] = tail
            prev1 = ubuf_ref[SUBLANES - 1:SUBLANES - 1 + tm, cols]
            prev2 = ubuf_ref[SUBLANES - 2:SUBLANES - 2 + tm, cols]
        else:
            ubuf_ref[:SUBLANES, cols] = jnp.zeros((SUBLANES, MIX_CHUNK), F32)
            st_ref[:, cols] = u
            t = lax.broadcasted_iota(jnp.int32, u.shape, 0) % seg_rows
            prev1 = jnp.where(t >= 1, ubuf_ref[SUBLANES - 1:SUBLANES - 1 + tm, cols], p1_ref[:, cols])
            prev2 = jnp.where(t >= 2, ubuf_ref[SUBLANES - 2:SUBLANES - 2 + tm, cols], p2_ref[:, cols])
        conv = cw_ref[0:1, cols] * prev2 + cw_ref[1:2, cols] * prev1 + cw_ref[2:3, cols] * u
        gated = (_dot(h, wb_ref[:, cols]) * conv).astype(BF16)
        part = _dot(gated, wo_ref[cols, :])
        y = part if y is None else y + part
    o_ref[...] += y * gt_ref[...]


def _conv_call(st, x, mods, norm_g, w_in, conv_w, w_out, layer, j_conv, prefix=None, seg_rows=None):
    if st.tile > MIX_ROWS:
        st = _Stream(st.rows, MIX_ROWS, st.tiles_per_seq * (st.tile // MIX_ROWS), st.mod_rows)
    tn = MIX_TILE
    nj = D_MODEL // tn
    n_tiles = st.rows // st.tile
    col = pl.BlockSpec((st.tile, tn), lambda i, j: (i, j))
    in_specs = [
        _x_spec(st),
        _norm_spec(layer, 1),
        _mod_spec(st, 3), _mod_spec(st, 4), _mod_spec(st, 5),
        pl.BlockSpec((None, D_MODEL, tn), lambda i, j: (j_conv, 0, j)),
        pl.BlockSpec((None, D_MODEL, tn), lambda i, j: (j_conv, 0, nj + j)),
        pl.BlockSpec((None, D_MODEL, tn), lambda i, j: (j_conv, 0, 2 * nj + j)),
        pl.BlockSpec((None, CONV_WIDTH, tn), lambda i, j: (j_conv, 0, j)),
        pl.BlockSpec((None, tn, D_MODEL), lambda i, j: (j_conv, j, 0)),
    ]
    args = [x, norm_g, mods, mods, mods, w_in, w_in, w_in, conv_w, w_out]
    scratch = [pltpu.VMEM((st.tile, D_MODEL), BF16), pltpu.VMEM((st.tile + SUBLANES, tn), F32)]
    if prefix is None:
        st_shape = jax.ShapeDtypeStruct((n_tiles, SUBLANES, D_MODEL), F32)
        st_spec = pl.BlockSpec((None, SUBLANES, tn), lambda i, j: (i, 0, j))
        scratch.append(pltpu.VMEM((nj, SUBLANES, tn), F32))
    else:
        in_specs += [col, col]
        args += list(prefix)
        st_shape = jax.ShapeDtypeStruct((st.rows, D_MODEL), F32)
        st_spec = col
    return pl.pallas_call(
        functools.partial(_conv_kernel, seg_rows=seg_rows, tiles_per_seq=st.tiles_per_seq),
        out_shape=(jax.ShapeDtypeStruct((st.rows, D_MODEL), F32), st_shape),
        grid=(n_tiles, nj),
        in_specs=in_specs,
        out_specs=(_x_spec(st), st_spec),
        scratch_shapes=scratch,
        compiler_params=_params(("arbitrary", "arbitrary")),
        name="conv_mixer",
    )(*args)


def _rope_tile(x, cos, sin, first_half):
    outs = []
    for c in range(x.shape[1] // LANES):
        xc = x[:, c * LANES:(c + 1) * LANES]
        partner = jnp.where(first_half, pltpu.roll(xc, LANES - HEAD_DIM // 2, 1),
                            pltpu.roll(xc, HEAD_DIM // 2, 1))
        outs.append(xc * cos + partner * sin)
    return jnp.concatenate(outs, axis=1)


def _qkv_kernel(x_ref, g_ref, sh_ref, sc_ref, wq_ref, wk_ref, wv_ref, cos_ref, sin_ref,
                q_ref, k_ref, vt_ref, kt_ref, vf_ref, h_ref):
    @pl.when(pl.program_id(1) == 0)
    def _():
        _norm_mod_rows(x_ref, g_ref, sh_ref, sc_ref, h_ref)

    h = h_ref[...]
    cos = cos_ref[...]
    sin = sin_ref[...]
    first_half = lax.broadcasted_iota(jnp.int32, cos.shape, 1) % HEAD_DIM < HEAD_DIM // 2
    q = _rope_tile(_dot(h, wq_ref[...]), cos, sin, first_half)
    k = _rope_tile(_dot(h, wk_ref[...]), cos, sin, first_half)
    v = _dot(h, wv_ref[...])
    q_ref[...] = q
    k_ref[...] = k.astype(BF16)
    kt_ref[...] = k.T
    vt_ref[...] = v.T.astype(BF16)
    vf_ref[...] = v


def _qkv_call(st, x, mods, norm_g, w_qkv, cos_t, sin_t, layer, j_attn):
    tn = QKV_TILE
    nj = D_MODEL // tn
    tps = st.tiles_per_seq
    seq_rows = st.tile * tps
    n_seq = st.rows // seq_rows
    col = pl.BlockSpec((st.tile, tn), lambda i, j: (i, j))
    col_t = pl.BlockSpec((None, tn, st.tile), lambda i, j: (i // tps, j, i % tps))
    rope = pl.BlockSpec((st.tile, LANES), lambda i, j: (i % tps, 0))
    return pl.pallas_call(
        _qkv_kernel,
        out_shape=(
            jax.ShapeDtypeStruct((st.rows, D_MODEL), F32),
            jax.ShapeDtypeStruct((st.rows, D_MODEL), BF16),
            jax.ShapeDtypeStruct((n_seq, D_MODEL, seq_rows), BF16),
            jax.ShapeDtypeStruct((n_seq, D_MODEL, seq_rows), F32),
            jax.ShapeDtypeStruct((st.rows, D_MODEL), F32),
        ),
        grid=(st.rows // st.tile, nj),
        in_specs=[
            _x_spec(st),
            _norm_spec(layer, 1),
            _mod_spec(st, 3), _mod_spec(st, 4),
            pl.BlockSpec((None, D_MODEL, tn), lambda i, j: (j_attn, 0, j)),
            pl.BlockSpec((None, D_MODEL, tn), lambda i, j: (j_attn, 0, nj + j)),
            pl.BlockSpec((None, D_MODEL, tn), lambda i, j: (j_attn, 0, 2 * nj + j)),
            rope, rope,
        ],
        out_specs=(col, col, col_t, col_t, col),
        scratch_shapes=[pltpu.VMEM((st.tile, D_MODEL), BF16)],
        compiler_params=_params(("arbitrary", "arbitrary")),
        name="qkv_rope",
    )(x, norm_g, mods, mods, w_qkv, w_qkv, w_qkv, cos_t, sin_t)


def _proj_kernel(a_ref, w_ref, x_ref, gt_ref, o_ref):
    o_ref[...] = x_ref[...] + gt_ref[...] * _dot(a_ref[...], w_ref[...])


def _proj_call(st, a, w_out, x, mods, j_attn):
    if st.tile > PROJ_ROWS:
        st = _Stream(st.rows, PROJ_ROWS, st.tiles_per_seq * (st.tile // PROJ_ROWS), st.mod_rows)
    tn = PROJ_TILE
    col = pl.BlockSpec((st.tile, tn), lambda i, j: (i, j))
    return pl.pallas_call(
        _proj_kernel,
        out_shape=jax.ShapeDtypeStruct((st.rows, D_MODEL), F32),
        grid=(st.rows // st.tile, D_MODEL // tn),
        in_specs=[
            pl.BlockSpec((st.tile, D_MODEL), lambda i, j: (i, 0)),
            pl.BlockSpec((None, D_MODEL, tn), lambda i, j: (j_attn, 0, j)),
            col,
            _mod_spec(st, 5, width=tn, by_col=True),
        ],
        out_specs=col,
        compiler_params=_params(("arbitrary", "arbitrary")),
        name="out_proj",
    )(a, w_out, x, mods)


def _subln(o, g, lam_init):
    return o * lax.rsqrt(jnp.mean(o * o, axis=-1, keepdims=True) + EPS) * g * (1.0 - lam_init)


def _flash_part(qi, sub, per_block, lam_ref, q_ref, k_ref, vt_ref, sg_ref, o_ref, qt_ref, s_ref, m_ref, l_ref,
                acc_ref, lam_init):
    t = ATTN_TILE
    scale = HEAD_DIM ** -0.5 * math.log2(math.e)

    def scores(ki):
        k0 = pl.multiple_of(ki * t, t)
        return _dot(k_ref[pl.ds(k0, t), :], qt_ref[...])

    def update(st, ki, diagonal):
        if diagonal:
            key = lax.broadcasted_iota(jnp.int32, st.shape, 0)
            qry = lax.broadcasted_iota(jnp.int32, st.shape, 1) % t
            st = jnp.where(key <= qry, st, NEG_INF)
        k0 = pl.multiple_of(ki * t, t)
        m_prev = m_ref[...]
        m_new = jnp.maximum(m_prev, jnp.max(st, axis=0, keepdims=True))
        alpha = jnp.exp2(m_prev - m_new)
        p = jnp.exp2(st - m_new)
        l_ref[...] = alpha * l_ref[...] + jnp.sum(p, axis=0, keepdims=True)
        acc_ref[...] = alpha * acc_ref[...] + _dot(vt_ref[:, pl.ds(k0, t)], p.astype(BF16))
        m_ref[...] = m_new

    def off_diagonal(src, dst, ki):
        nxt = scores(ki + 1)
        update(s_ref[src], ki, False)
        s_ref[dst] = nxt

    def off_diagonal_pair(j, carry):
        off_diagonal(0, 1, 2 * j)
        off_diagonal(1, 0, 2 * j + 1)
        return carry

    def finish(src):
        update(s_ref[src], qi, True)
        ot = acc_ref[...] * (1.0 / l_ref[...])
        d = ot[:, :t] - lam_ref[0] * ot[:, t:]
        d = d * lax.rsqrt(jnp.mean(d * d, axis=0, keepdims=True) + EPS)
        o_ref[pl.ds(q0, t), :] = (d.T * sg_ref[...] * (1.0 - lam_init)).astype(BF16)

    q0 = pl.multiple_of(qi * t, t)

    @pl.when(sub == 0)
    def _():
        q = q_ref[pl.ds(q0, t), :] * scale
        lane = lax.broadcasted_iota(jnp.int32, q.shape, 1)
        qt_ref[:, :t] = jnp.where(lane < HEAD_DIM, q, 0.0).T.astype(BF16)
        qt_ref[:, t:] = jnp.where(lane >= HEAD_DIM, q, 0.0).T.astype(BF16)
        m_ref[...] = jnp.full(m_ref.shape, NEG_INF, F32)
        l_ref[...] = jnp.zeros(l_ref.shape, F32)
        acc_ref[...] = jnp.zeros(acc_ref.shape, F32)
        s_ref[0] = scores(0)

    lo = qi * sub // per_block
    hi = qi * (sub + 1) // per_block
    even_lo = lo + lo % 2

    @pl.when(jnp.logical_and(lo % 2 == 1, lo < hi))
    def _():
        off_diagonal(1, 0, lo)

    lax.fori_loop(even_lo // 2, jnp.maximum(hi, even_lo) // 2, off_diagonal_pair, 0)

    @pl.when(jnp.logical_and(hi % 2 == 1, hi > even_lo))
    def _():
        off_diagonal(0, 1, hi - 1)

    last = sub == per_block - 1

    @pl.when(jnp.logical_and(last, qi % 2 == 0))
    def _():
        finish(0)

    @pl.when(jnp.logical_and(last, qi % 2 == 1))
    def _():
        finish(1)


def _attn_kernel(pt_ref, plan_ref, lam_ref, qs_ref, kn_ref, vn_ref, sg_ref, fq_ref, fk_ref, fvt_ref,
                 *rest, n_pages, t_new, lam_init):
    pp = PAGES_PER_STEP
    k_refs = rest[:pp]
    v_refs = rest[pp:2 * pp]
    (o_ref, fo_ref, e_ref, hm_ref, qbd_ref, s_ref, a_ref, acc_ref,
     qt_ref, fs_ref, fm_ref, fl_ref, facc_ref) = rest[2 * pp:]
    step = pl.program_id(1)
    n_steps = 2 * (n_pages // pp)
    _flash_part(plan_ref[step], plan_ref[n_steps + step], plan_ref[2 * n_steps + step],
                lam_ref, fq_ref, fk_ref, fvt_ref, sg_ref, fo_ref,
                qt_ref, fs_ref, fm_ref, fl_ref, facc_ref, lam_init)
    k_steps = n_pages // pp
    past = n_pages * PAGE_SIZE
    scale = HEAD_DIM ** -0.5
    n_rows = N_HEADS * t_new

    @pl.when(jnp.logical_and(pl.program_id(0) == 0, step == 0))
    def _():
        tok = lax.broadcasted_iota(jnp.int32, e_ref.shape, 0)
        lane = lax.broadcasted_iota(jnp.int32, e_ref.shape, 1)
        e_ref[...] = jnp.where(lane // N_HEADS == tok, 1.0, 0.0).astype(BF16)
        row = lax.broadcasted_iota(jnp.int32, hm_ref.shape, 0)
        lane = lax.broadcasted_iota(jnp.int32, hm_ref.shape, 1)
        hm_ref[...] = jnp.where(lane % N_HEADS == row // t_new, 1.0, 0.0)

    @pl.when(step == 0)
    def _():
        row = lax.broadcasted_iota(jnp.int32, qbd_ref.shape, 0)
        lane = lax.broadcasted_iota(jnp.int32, qbd_ref.shape, 1)
        q = jnp.zeros(qbd_ref.shape, F32)
        for tq in range(t_new):
            q = jnp.where(row % t_new == tq, qs_ref[tq:tq + 1, :], q)
        q_head = 2 * (row % n_rows // t_new) + row // n_rows
        qbd_ref[...] = jnp.where(lane // HEAD_DIM == q_head, q, 0.0)

    @pl.when(step < k_steps)
    def _():
        qbd = qbd_ref[...]
        for i in range(pp):
            col = pl.multiple_of((step * pp + i) * PAGE_SIZE, PAGE_SIZE)
            s_ref[:, pl.ds(col, PAGE_SIZE)] = _dot(qbd, k_refs[i][...]) * scale

    @pl.when(step == k_steps - 1)
    def _():
        sn = lax.dot_general(qbd_ref[...].astype(BF16), kn_ref[...], (((1,), (1,)), ((), ())),
                             preferred_element_type=F32) * scale
        tq = lax.broadcasted_iota(jnp.int32, sn.shape, 0) % t_new
        tk = lax.broadcasted_iota(jnp.int32, sn.shape, 1)
        s_ref[:, past:] = jnp.where(tk <= tq, sn, NEG_INF)
        sc = s_ref[...]
        p = jnp.exp(sc - jnp.max(sc, axis=1, keepdims=True))
        p = p * (1.0 / jnp.sum(p, axis=1, keepdims=True))
        a_ref[...] = (p[:n_rows] - lam_ref[0] * p[n_rows:]).astype(BF16)
        acc_ref[...] = jnp.zeros(acc_ref.shape, F32)

    @pl.when(step >= k_steps)
    def _():
        cols = [pl.multiple_of(((step - k_steps) * pp + i) * PAGE_SIZE, PAGE_SIZE) for i in range(pp)]
        a = jnp.concatenate([a_ref[:, pl.ds(c, PAGE_SIZE)] for c in cols], axis=0)
        spread = _dot(a, e_ref[...])
        acc = acc_ref[...]
        for i in range(pp):
            acc += _dot(spread[i * n_rows:(i + 1) * n_rows] * hm_ref[...], v_refs[i][...])
        acc_ref[...] = acc

    @pl.when(step == 2 * k_steps - 1)
    def _():
        spread = _dot(a_ref[:, past:], e_ref[...]) * hm_ref[...]
        o = acc_ref[...] + _dot(spread[:, :vn_ref.shape[0]], vn_ref[...])
        o_ref[...] = _subln(o, sg_ref[...], lam_init).astype(BF16)


def _attn_call(page_table, lam, q_new, k_new, v_new, subln_g, k_pages, v_pages,
               q_p, k_p, vt_p, lam_init):
    n_seq, n_pages = page_table.shape
    pp = PAGES_PER_STEP
    k_steps = n_pages // pp
    t_new = q_new.shape[1]
    n_rows = N_HEADS * t_new
    keys = n_pages * PAGE_SIZE + LANES
    page_rows = k_pages.shape[1]
    n_prompt, _, seq_len = vt_p.shape
    t = ATTN_TILE
    n_steps = 2 * k_steps
    n_blocks = seq_len // t
    assert n_seq == n_prompt * N_HEADS and n_steps >= n_blocks
    weight = [qi + 2 for qi in range(n_blocks)]
    parts = [max(1, round(w * n_steps / sum(weight))) for w in weight]
    while sum(parts) > n_steps:
        parts[parts.index(max(parts))] -= 1
    parts[-1] += n_steps - sum(parts)
    plan = ([qi for qi, n in enumerate(parts) for _ in range(n)]
            + [sub for n in parts for sub in range(n)]
            + [n for n in parts for _ in range(n)])

    def head_rows():
        return pl.BlockSpec((seq_len, LANES), lambda b, s, pt, plan: (b // N_HEADS, b % N_HEADS),
                            pipeline_mode=pl.Buffered(1))

    def k_spec(i):
        return pl.BlockSpec(
            (None, page_rows, PAGE_SIZE),
            lambda b, s, pt, plan: (pt[b * n_pages + jnp.minimum(s, k_steps - 1) * pp + i], 0, 0))

    def v_spec(i):
        return pl.BlockSpec(
            (None, page_rows, V_DIM),
            lambda b, s, pt, plan: (pt[b * n_pages + jnp.maximum(s - k_steps, 0) * pp + i], 0, 0))

    def whole(shape):
        return pl.BlockSpec(shape, lambda b, s, pt, plan: (0,) * len(shape))

    def per_seq(shape):
        return pl.BlockSpec((None,) + shape, lambda b, s, pt, plan: (b,) + (0,) * len(shape))

    grid_spec = pltpu.PrefetchScalarGridSpec(
        num_scalar_prefetch=2,
        grid=(n_seq, n_steps),
        in_specs=[
            pl.BlockSpec(memory_space=pltpu.SMEM),
            per_seq((t_new, D_MODEL)),
            per_seq((LANES, D_MODEL)),
            per_seq((LANES, V_DIM)),
            whole((1, V_DIM)),
            head_rows(), head_rows(),
            pl.BlockSpec((None, V_DIM, seq_len), lambda b, s, pt, plan: (b // N_HEADS, b % N_HEADS, 0),
                         pipeline_mode=pl.Buffered(1)),
        ] + [k_spec(i) for i in range(pp)] + [v_spec(i) for i in range(pp)],
        out_specs=(
            per_seq((n_rows, V_DIM)),
            pl.BlockSpec((seq_len, LANES), lambda b, s, pt, plan: (b // N_HEADS, b % N_HEADS)),
        ),
        scratch_shapes=[
            pltpu.VMEM((PAGE_SIZE, PAGE_SIZE * N_HEADS), BF16), pltpu.VMEM((n_rows, PAGE_SIZE * N_HEADS), F32),
            pltpu.VMEM((2 * n_rows, D_MODEL), F32),
            pltpu.VMEM((2 * n_rows, keys), F32), pltpu.VMEM((n_rows, keys), BF16),
            pltpu.VMEM((n_rows, V_DIM), F32),
            pltpu.VMEM((LANES, 2 * t), BF16), pltpu.VMEM((2, t, 2 * t), F32),
            pltpu.VMEM((1, 2 * t), F32), pltpu.VMEM((1, 2 * t), F32), pltpu.VMEM((V_DIM, 2 * t), F32),
        ],
    )
    return pl.pallas_call(
        functools.partial(_attn_kernel, n_pages=n_pages, t_new=t_new, lam_init=lam_init),
        out_shape=(
            jax.ShapeDtypeStruct((n_seq, n_rows, V_DIM), BF16),
            jax.ShapeDtypeStruct(q_p.shape, BF16),
        ),
        grid_spec=grid_spec,
        compiler_params=_params(("arbitrary", "arbitrary")),
        name="diff_attn",
    )(page_table.reshape(-1), jnp.asarray(plan, jnp.int32), lam, q_new, k_new, v_new, subln_g, q_p, k_p, vt_p,
      *([k_pages] * pp), *([v_pages] * pp))


def _rope_tables(pos):
    half = HEAD_DIM // 2
    inv = ROPE_THETA ** (-jnp.arange(half, dtype=F32) / half)
    ang = pos.astype(F32)[:, None] * inv[None, :]
    cos = jnp.cos(ang)
    sin = jnp.sin(ang)
    reps = LANES // HEAD_DIM
    return (jnp.tile(jnp.concatenate([cos, cos], axis=1), (1, reps)),
            jnp.tile(jnp.concatenate([-sin, sin], axis=1), (1, reps)))


def _attention(q, k, v, q_p, k_p, vt_p, page_table, lam, subln_g, k_pages, v_pages, lam_init):
    n_seq = page_table.shape[0]
    t_new = q.shape[0] // n_seq
    q_new = q.reshape(n_seq, t_new, D_MODEL)
    k_new = jnp.pad(k.reshape(n_seq, t_new, D_MODEL), ((0, 0), (0, LANES - t_new), (0, 0)))
    v_new = jnp.pad(v.reshape(n_seq, t_new * N_HEADS, V_DIM), ((0, 0), (0, LANES - t_new * N_HEADS), (0, 0)))
    o, o_p = _attn_call(page_table, lam, q_new, k_new, v_new, subln_g,
                        k_pages, v_pages, q_p, k_p, vt_p, lam_init)
    o = o.reshape(n_seq, N_HEADS, t_new, V_DIM).transpose(0, 2, 1, 3)
    return o.reshape(n_seq * t_new, D_MODEL), o_p


def _to_attention(st, x, mods, pos_tables, p, conv_prefix, seg_rows):
    cos_t, sin_t = pos_tables
    x = _ffn_call(st, x, mods[0], p['norm_g'], p['w_gu'], p['w_d'], 0, 0)
    x, conv_state = _conv_call(st, x, mods[0], p['norm_g'], p['conv_w_in'], p['conv_w'], p['conv_w_out'],
                               0, 0, conv_prefix, seg_rows)
    x = _ffn_call(st, x, mods[0], p['norm_g'], p['w_gu'], p['w_d'], 0, 1)
    x = _ffn_call(st, x, mods[1], p['norm_g'], p['w_gu'], p['w_d'], 1, 0)
    return (x, conv_state) + _qkv_call(st, x, mods[1], p['norm_g'], p['attn_w_qkv'], cos_t, sin_t, 1, 0)


def _from_attention(st, x, o, mods, p):
    x = _proj_call(st, o, p['attn_w_out'], x, mods[1], 0)
    return _ffn_call(st, x, mods[1], p['norm_g'], p['w_gu'], p['w_d'], 1, 1, p['final_g'])


def _k_from_transposed(kt, n_seq, rows_per_seq):
    s, _, rows = kt.shape
    k = kt.reshape(s, 2 * N_HEADS, HEAD_DIM, rows).transpose(0, 3, 1, 2)
    return k.reshape(1, n_seq, rows_per_seq, 2 * N_HEADS, HEAD_DIM)


def kernel(x_prompt, x_sample, c_prompt, c_sample, state_conv, cache_k, cache_v, page_table, norm_g, final_g,
           w_ada, b_ada, ffn_w_gate, ffn_w_up, ffn_w_down, conv_w_in, conv_w, conv_w_out, attn_w_qkv,
           attn_w_out, lambda_q1, lambda_k1, lambda_q2, lambda_k2, subln_g):
    n_prompt, seq_len, _ = x_prompt.shape
    n_sample, t_new, _ = x_sample.shape
    n_pages = page_table.shape[1]
    assert DEPTH == 2 and t_new >= CONV_WIDTH - 1 and seq_len % PROMPT_TILE == 0
    w_gu, w_d = _cast_ffn_weights(ffn_w_gate, ffn_w_up, ffn_w_down)
    p = dict(
        norm_g=norm_g.reshape(DEPTH * 3, 1, D_MODEL),
        final_g=final_g,
        w_gu=w_gu,
        w_d=w_d,
        conv_w_in=conv_w_in.astype(BF16),
        conv_w=conv_w,
        conv_w_out=conv_w_out.astype(BF16),
        attn_w_qkv=attn_w_qkv.astype(BF16),
        attn_w_out=attn_w_out.astype(BF16),
    )
    j_attn = 0
    lam = (jnp.exp(jnp.sum(lambda_q1[j_attn] * lambda_k1[j_attn]))
           - jnp.exp(jnp.sum(lambda_q2[j_attn] * lambda_k2[j_attn])) + _lambda_init(1)).reshape(1).astype(F32)
    sg = subln_g[j_attn].reshape(1, V_DIM)

    n_cond = n_prompt + n_sample
    c_rows = ((n_cond + SUBLANES - 1) // SUBLANES) * SUBLANES
    c_all = jnp.pad(jnp.concatenate([c_prompt, c_sample], axis=0), ((0, c_rows - n_cond), (0, 0)))
    mods = _ada_call(c_all, w_ada, b_ada).reshape(DEPTH, c_rows, N_ADA, D_MODEL)
    mods_p = [mods[l, :n_prompt].transpose(1, 0, 2)[:, :, None, :] for l in range(DEPTH)]
    mods_s = [jnp.repeat(mods[l, n_prompt:n_cond], t_new, axis=0).transpose(1, 0, 2)[:, None]
              for l in range(DEPTH)]

    st_p = _Stream(n_prompt * seq_len, PROMPT_TILE, seq_len // PROMPT_TILE, 1)
    pos_p = _rope_tables(jnp.arange(seq_len, dtype=jnp.int32))
    x_p, cs_p, q_p, k_p, vt_p, kt_p, v_p = _to_attention(
        st_p, x_prompt.reshape(-1, D_MODEL), mods_p, pos_p, p, None, None)

    rows_s = n_sample * t_new
    st_s = _Stream(rows_s, rows_s, 1, rows_s)
    pos_s = _rope_tables(n_pages * PAGE_SIZE + jnp.arange(rows_s, dtype=jnp.int32) % t_new)
    t_idx = (jnp.arange(rows_s) % t_new)[:, None]
    pre0 = jnp.repeat(state_conv[0, :, 0], t_new, axis=0)
    pre1 = jnp.repeat(state_conv[0, :, 1], t_new, axis=0)
    prefix1 = jnp.where(t_idx == 0, pre1, 0.0)
    prefix2 = jnp.where(t_idx == 0, pre0, jnp.where(t_idx == 1, pre1, 0.0))
    k_pages = cache_k[j_attn].transpose(0, 2, 3, 1).reshape(-1, 2 * N_HEADS * HEAD_DIM, PAGE_SIZE)
    v_pages = cache_v[j_attn].reshape(-1, PAGE_SIZE * N_HEADS, V_DIM)
    x_s, u_s, q_s, k_s, _, kt_s, v_s = _to_attention(
        st_s, x_sample.reshape(-1, D_MODEL), mods_s, pos_s, p, (prefix1, prefix2), t_new)

    o_s, o_p = _attention(q_s, k_s, v_s, q_p, k_p, vt_p, page_table, lam, sg,
                          k_pages, v_pages, _lambda_init(1))
    y_p = _from_attention(st_p, x_p, o_p, mods_p, p)
    y_s = _from_attention(st_s, x_s, o_s, mods_s, p)

    keep = CONV_WIDTH - 1
    cs_p = cs_p.reshape(n_prompt, -1, SUBLANES, D_MODEL)[:, -1, SUBLANES - keep:, :]
    return (
        y_p.reshape(n_prompt, seq_len, D_MODEL),
        y_s.reshape(n_sample, t_new, D_MODEL),
        cs_p[None],
        u_s.reshape(n_sample, t_new, D_MODEL)[None, :, t_new - keep:, :],
        _k_from_transposed(kt_p, n_prompt, seq_len),
        v_p.reshape(1, n_prompt, seq_len, N_HEADS, V_DIM),
        _k_from_transposed(kt_s, n_sample, t_new),
        v_s.reshape(1, n_sample, t_new, N_HEADS, V_DIM),
    )
```
